```python
import math
import jax, jax.numpy as jnp
from jax import lax
import numpy as np

D_MODEL = 2048
BATCH = 2
SEQ = 16384
DEPTH = 2

GRID_W = 64
CTX_LEN = 256

POOL_WIDTH = D_MODEL // 4
CONV_WIDTH = D_MODEL // 4
NA_WIDTH = D_MODEL - POOL_WIDTH - CONV_WIDTH
MIX_WIDTH = POOL_WIDTH + NA_WIDTH + CONV_WIDTH
POOL_WINDOWS = (2, 4, 8, 16)
POOL_GROUP = POOL_WIDTH // len(POOL_WINDOWS)
NA_HEAD_DIM = 64
NA_HEADS = NA_WIDTH // NA_HEAD_DIM
NA_KH = 8
NA_KW = 16
ROPE_THETA = 10000.0
CONV_K = 31
NORM_EPS = 1e-6
LN_EPS = 1e-5

OFF_A_V = 0
OFF_A_G = OFF_A_V + POOL_WIDTH
OFF_B_Q = OFF_A_G + POOL_WIDTH
OFF_B_K = OFF_B_Q + NA_WIDTH
OFF_B_V = OFF_B_K + NA_WIDTH
OFF_B_G = OFF_B_V + NA_WIDTH
OFF_C_U = OFF_B_G + NA_WIDTH
OFF_C_G = OFF_C_U + 2 * CONV_WIDTH
PROJ_WIDTH = OFF_C_G + CONV_WIDTH

kernel_name = "hybrid_pool_natten_conformer_dit"


def rms_norm(x, g):
    xf = x.astype(jnp.float32)
    y = xf * lax.rsqrt(jnp.mean(xf * xf, axis=-1, keepdims=True) + NORM_EPS)
    return (y * g.astype(jnp.float32)).astype(x.dtype)


def to_heads(t):
    return t.reshape(t.shape[:-1] + (NA_HEADS, NA_HEAD_DIM))


def axial_rope(x, rows_pos, cols_pos):
    half = x.shape[-1] // 2
    freqs = ROPE_THETA ** (-jnp.arange(0, half, 2, dtype=jnp.float32) / half)

    def rot(xp, pos):
        ang = pos.astype(jnp.float32)[:, None] * freqs[None, :]
        cos = jnp.cos(ang)[None, :, None, :]
        sin = jnp.sin(ang)[None, :, None, :]
        xf = xp.astype(jnp.float32)
        x1, x2 = xf[..., : half // 2], xf[..., half // 2:]
        return jnp.concatenate([x1 * cos - x2 * sin, x1 * sin + x2 * cos], axis=-1)

    out = jnp.concatenate([rot(x[..., :half], rows_pos), rot(x[..., half:], cols_pos)], axis=-1)
    return out.astype(x.dtype)


def multiscale_pool(u, w_pool, pool_scale):
    B, L, _ = u.shape
    uf = u.astype(jnp.float32)
    cs = jnp.concatenate([jnp.zeros((B, 1, POOL_WIDTH), jnp.float32), jnp.cumsum(uf, axis=1)], axis=1)
    t = jnp.arange(L)
    means = []
    for g, w in enumerate(POOL_WINDOWS):
        lo = jnp.clip(t - w // 2, 0, L - 1)
        hi = jnp.clip(t + (w - w // 2) - 1, 0, L - 1)
        seg = cs[..., g * POOL_GROUP:(g + 1) * POOL_GROUP]
        total = jnp.take(seg, hi + 1, axis=1) - jnp.take(seg, lo, axis=1)
        means.append(total / (hi - lo + 1).astype(jnp.float32)[None, :, None])
    pooled = (jnp.concatenate(means, axis=-1) - uf).astype(u.dtype)
    y = jnp.einsum('blgc,gcd->blgd', pooled.reshape(B, L, len(POOL_WINDOWS), POOL_GROUP), w_pool)
    return y.reshape(B, L, POOL_WIDTH) * pool_scale


def neighbourhood_attention(q, k, v, kc, vc, rpb):
    B, L, H, hd = q.shape
    rows = L // GRID_W
    kh = min(NA_KH, rows)
    scale = hd ** -0.5
    qg = jnp.moveaxis((q * scale).reshape(B, rows, GRID_W, H, hd), 1, 0)
    kg = k.reshape(B, rows, GRID_W, H, hd)
    vg = v.reshape(B, rows, GRID_W, H, hd)
    kc_s = kc
    col = np.arange(GRID_W)
    col_start = np.clip(col - NA_KW // 2, 0, GRID_W - NA_KW)
    col_idx = col_start[:, None] + np.arange(NA_KW)[None, :]
    dc_idx = col_idx - col[:, None] + NA_KW - 1
    n_loc = kh * NA_KW

    def one_row(args):
        r, q_r = args
        rs = jnp.clip(r - kh // 2, 0, rows - kh)
        k_rows = lax.dynamic_slice_in_dim(kg, rs, kh, axis=1)
        v_rows = lax.dynamic_slice_in_dim(vg, rs, kh, axis=1)
        k_win = k_rows[:, :, col_idx]
        v_win = v_rows[:, :, col_idx]
        dr_idx = rs + jnp.arange(kh) - r + NA_KH - 1
        bias = rpb[:, dr_idx[None, :, None], dc_idx[:, None, :]]
        s_loc = jnp.einsum('bwhd,bawkhd->bhwak', q_r, k_win).astype(jnp.float32) + bias[None].astype(jnp.float32)
        s_ctx = jnp.einsum('bwhd,bchd->bhwc', q_r, kc_s).astype(jnp.float32)
        s = jnp.concatenate([s_loc.reshape(B, H, GRID_W, n_loc), s_ctx], axis=-1)
        p = jax.nn.softmax(s, axis=-1).astype(v.dtype)
        p_loc = p[..., :n_loc].reshape(B, H, GRID_W, kh, NA_KW)
        p_ctx = p[..., n_loc:]
        return (jnp.einsum('bhwak,bawkhd->bwhd', p_loc, v_win)
                + jnp.einsum('bhwc,bchd->bwhd', p_ctx, vc))

    out = lax.map(one_row, (jnp.arange(rows), qg))
    return jnp.moveaxis(out, 0, 1).reshape(B, L, H * hd)


def context_attention(qc, kc, vc):
    B, Lc, H, hd = qc.shape
    s = jnp.einsum('bqhd,bchd->bhqc', qc * (hd ** -0.5), kc).astype(jnp.float32)
    p = jax.nn.softmax(s, axis=-1).astype(vc.dtype)
    return jnp.einsum('bhqc,bchd->bqhd', p, vc).reshape(B, Lc, H * hd)


def conformer_conv(u, conv_dw, conv_dw_b, ln_g, ln_b, conv_pw, conv_pw_b):
    y = u[..., :CONV_WIDTH] * jax.nn.sigmoid(u[..., CONV_WIDTH:])
    y = lax.conv_general_dilated(
        y, conv_dw[:, None, :], window_strides=(1,),
        padding=((CONV_K // 2, CONV_K // 2),),
        dimension_numbers=('NWC', 'WIO', 'NWC'),
        feature_group_count=CONV_WIDTH) + conv_dw_b
    yf = y.astype(jnp.float32)
    mu = jnp.mean(yf, axis=-1, keepdims=True)
    var = jnp.mean(jnp.square(yf - mu), axis=-1, keepdims=True)
    yn = ((yf - mu) * lax.rsqrt(var + LN_EPS) * ln_g.astype(jnp.float32) + ln_b.astype(jnp.float32)).astype(y.dtype)
    return jax.nn.silu(yn) @ conv_pw + conv_pw_b


def mix_out(p, na_out, w_pool, pool_scale, conv_dw, conv_dw_b, conv_ln_g, conv_ln_b, conv_pw, conv_pw_b, w_out):
    ya = multiscale_pool(p[..., OFF_A_V:OFF_A_G], w_pool, pool_scale) * jax.nn.silu(p[..., OFF_A_G:OFF_B_Q])
    yb = na_out * jax.nn.silu(p[..., OFF_B_G:OFF_C_U])
    yc = conformer_conv(p[..., OFF_C_U:OFF_C_G], conv_dw, conv_dw_b, conv_ln_g, conv_ln_b,
                        conv_pw, conv_pw_b) * jax.nn.silu(p[..., OFF_C_G:PROJ_WIDTH])
    return jnp.concatenate([ya, yb, yc], axis=-1) @ w_out


def hybrid_layer(x, xc, c_act, cc_act, w_mod, b_mod, norm_g, w_in, w_pool, pool_scale, na_rpb,
                 conv_dw, conv_dw_b, conv_ln_g, conv_ln_b, conv_pw, conv_pw_b, w_out,
                 rows_pos, cols_pos, update_ctx):
    shift, scale, gate = jnp.split(c_act @ w_mod + b_mod, 3, axis=-1)
    shift_c, scale_c, gate_c = jnp.split(cc_act @ w_mod + b_mod, 3, axis=-1)

    hc = rms_norm(xc, norm_g) * (1 + scale_c) + shift_c
    if update_ctx:
        pc = hc @ w_in
        kv_c = pc[..., OFF_B_K:OFF_B_G]
    else:
        kv_c = hc @ w_in[:, OFF_B_K:OFF_B_G]
    kc = to_heads(kv_c[..., :NA_WIDTH])
    vc = to_heads(kv_c[..., NA_WIDTH:])

    h = rms_norm(x, norm_g) * (1 + scale[:, None, :]) + shift[:, None, :]
    p = h @ w_in
    q = axial_rope(to_heads(p[..., OFF_B_Q:OFF_B_K]), rows_pos, cols_pos)
    k = axial_rope(to_heads(p[..., OFF_B_K:OFF_B_V]), rows_pos, cols_pos)
    v = to_heads(p[..., OFF_B_V:OFF_B_G])
    na_lat = neighbourhood_attention(q, k, v, kc, vc, na_rpb)
    x_new = x + gate[:, None, :] * mix_out(p, na_lat, w_pool, pool_scale, conv_dw, conv_dw_b,
                                           conv_ln_g, conv_ln_b, conv_pw, conv_pw_b, w_out)
    if update_ctx:
        qc = to_heads(pc[..., OFF_B_Q:OFF_B_K])
        na_c = context_attention(qc, kc, vc)
        xc = xc + gate_c * mix_out(pc, na_c, w_pool, pool_scale, conv_dw, conv_dw_b,
                                   conv_ln_g, conv_ln_b, conv_pw, conv_pw_b, w_out)
    return x_new, xc


def setup_inputs(seed: int = 0) -> dict:
    key = jax.random.key(seed)
    ks = jax.random.split(key, 24)
    f32 = jnp.float32
    nrm = lambda k, shape, s: jax.random.normal(k, shape, f32) * s
    return {
        "x": nrm(ks[0], (BATCH, SEQ, D_MODEL), 1.0),
        "c": nrm(ks[1], (BATCH, D_MODEL), 1.0),
        "ctx": nrm(ks[2], (BATCH, CTX_LEN, D_MODEL), 1.0),
        "c_ctx": nrm(ks[3], (D_MODEL,), 1.0),
        "w_mod": nrm(ks[4], (DEPTH, D_MODEL, 3 * D_MODEL), 0.5 * D_MODEL ** -0.5),
        "b_mod": nrm(ks[5], (DEPTH, 3 * D_MODEL), 0.01),
        "norm_g": 1.0 + nrm(ks[6], (DEPTH, D_MODEL), 0.02),
        "w_in": nrm(ks[7], (DEPTH, D_MODEL, PROJ_WIDTH), D_MODEL ** -0.5),
        "w_pool": nrm(ks[8], (DEPTH, len(POOL_WINDOWS), POOL_GROUP, POOL_GROUP), POOL_GROUP ** -0.5),
        "pool_scale": 1.0 + nrm(ks[9], (DEPTH, POOL_WIDTH), 0.1),
        "na_rpb": nrm(ks[10], (DEPTH, NA_HEADS, 2 * NA_KH - 1, 2 * NA_KW - 1), 0.1),
        "conv_dw": nrm(ks[11], (DEPTH, CONV_K, CONV_WIDTH), CONV_K ** -0.5),
        "conv_dw_b": nrm(ks[12], (DEPTH, CONV_WIDTH), 0.01),
        "conv_ln_g": 1.0 + nrm(ks[13], (DEPTH, CONV_WIDTH), 0.02),
        "conv_ln_b": nrm(ks[14], (DEPTH, CONV_WIDTH), 0.01),
        "conv_pw": nrm(ks[15], (DEPTH, CONV_WIDTH, CONV_WIDTH), CONV_WIDTH ** -0.5),
        "conv_pw_b": nrm(ks[16], (DEPTH, CONV_WIDTH), 0.01),
        "w_out": nrm(ks[17], (DEPTH, MIX_WIDTH, D_MODEL), MIX_WIDTH ** -0.5),
        "final_norm_g": 1.0 + nrm(ks[18], (D_MODEL,), 0.02),
    }


def reference(x, c, ctx, c_ctx, w_mod, b_mod, norm_g, w_in, w_pool, pool_scale, na_rpb,
              conv_dw, conv_dw_b, conv_ln_g, conv_ln_b, conv_pw, conv_pw_b, w_out, final_norm_g):
    L = x.shape[1]
    t = jnp.arange(L)
    rows_pos = t // GRID_W
    cols_pos = t % GRID_W
    c_act = jax.nn.silu(c)
    cc_act = jax.nn.silu(c_ctx)
    xc = ctx
    for i in range(DEPTH):
        x, xc = hybrid_layer(
            x, xc, c_act, cc_act, w_mod[i], b_mod[i], norm_g[i], w_in[i], w_pool[i], pool_scale[i],
            na_rpb[i], conv_dw[i], conv_dw_b[i], conv_ln_g[i], conv_ln_b[i], conv_pw[i], conv_pw_b[i],
            w_out[i], rows_pos, cols_pos, i < DEPTH - 1)
    return rms_norm(x, final_norm_g)
```

```python
import functools

import numpy as np
import jax
import jax.numpy as jnp
from jax import lax
from jax.experimental import pallas as pl
from jax.experimental.pallas import tpu as pltpu

F32 = jnp.float32
BF16 = jnp.bfloat16

GRID_W = 64
POOL_WINDOWS = (2, 4, 8, 16)
HEAD_DIM = 64
NA_KH = 8
NA_KW = 16
ROPE_THETA = 10000.0
CONV_K = 31
NORM_EPS = 1e-6
LN_EPS = 1e-5

LANES = 128
BF16_SUBLANES = 16
VMEM_LIMIT_BYTES = 56 * 1024 * 1024

SEG = 512
HALO = 16
MASK_VALUE = -1e30
KEY_ROWS = 10
ATT_ROWS_PER_STEP = 8
CONV_CHUNK = 32


def _sigmoid(v):
    return 1.0 / (1.0 + jnp.exp(-v))


def _silu(v):
    return v * _sigmoid(v)


def _mod_kernel(c_ref, w_ref, b_ref, o_ref):
    a = _silu(c_ref[...]).astype(BF16)
    o_ref[0] = jnp.dot(a, w_ref[0].astype(BF16), preferred_element_type=F32) + b_ref[0]


def _mod_call(cstack, w_mod, b_mod):
    depth, d, n = w_mod.shape
    tn = 768
    return pl.pallas_call(
        _mod_kernel,
        grid=(depth, n // tn),
        in_specs=[
            pl.BlockSpec((8, d), lambda l, j: (0, 0)),
            pl.BlockSpec((1, d, tn), lambda l, j: (l, 0, j)),
            pl.BlockSpec((1, 1, tn), lambda l, j: (l, 0, j)),
        ],
        out_specs=pl.BlockSpec((1, 8, tn), lambda l, j: (l, 0, j)),
        out_shape=jax.ShapeDtypeStruct((depth, 8, n), F32),
        compiler_params=pltpu.CompilerParams(
            dimension_semantics=("parallel", "parallel"), vmem_limit_bytes=VMEM_LIMIT_BYTES),
        name="adaln_mod",
    )(cstack, w_mod, b_mod.reshape(depth, 1, n))


N_IN_STEPS = 13


def _rope(r, cos_ref, sin_ref):
    n = r.shape[1]
    reps = n // LANES
    lane = lax.broadcasted_iota(jnp.int32, r.shape, 1)
    up = pltpu.roll(r, n - 16, axis=1)
    dn = pltpu.roll(r, 16, axis=1)
    partner = jnp.where((lane % 32) < 16, up, dn)
    cos = jnp.tile(cos_ref[...], (1, reps))
    sin = jnp.tile(sin_ref[...], (1, reps))
    return r * cos + partner * sin


def _in_kernel(x_ref, scale_ref, shift_ref, g_ref, w_ref, cq_ref, sq_ref, ck_ref, sk_ref,
               o_ref, h_scr, ua_scr):
    j = pl.program_id(1)

    @pl.when(j == 0)
    def _():
        x = x_ref[...]
        ms = jnp.mean(x * x, axis=-1, keepdims=True)
        hn = x * lax.rsqrt(ms + NORM_EPS) * g_ref[...]
        h_scr[...] = (hn * (1.0 + scale_ref[0]) + shift_ref[0]).astype(BF16)

    r = jnp.dot(h_scr[...], w_ref[...], preferred_element_type=F32)

    @pl.when((j == 0) | (j == 6) | (j == 7))
    def _():
        o_ref[...] = r.astype(BF16)

    @pl.when((j == 1) | (j == 8) | (j == 9) | (j == 12))
    def _():
        o_ref[...] = _silu(r).astype(BF16)

    @pl.when((j == 2) | (j == 3))
    def _():
        o_ref[...] = _rope(r, cq_ref, sq_ref).astype(BF16)

    @pl.when((j == 4) | (j == 5))
    def _():
        o_ref[...] = _rope(r, ck_ref, sk_ref).astype(BF16)

    @pl.when(j == 10)
    def _():
        ua_scr[...] = r

    @pl.when(j == 11)
    def _():
        o_ref[...] = (ua_scr[...] * _sigmoid(r)).astype(BF16)


def _in_call(x2d, mod, mod_row, norm_g, w_in, tabs, seq_len, tm):
    n_tok, d = x2d.shape
    seq_tiles = seq_len // tm
    if mod_row is None:
        row = lambda i: i // seq_tiles
    else:
        row = lambda i: mod_row
    tab_spec = pl.BlockSpec((tm, LANES), lambda i, j: (i % seq_tiles, 0))
    return pl.pallas_call(
        _in_kernel,
        grid=(n_tok // tm, N_IN_STEPS),
        in_specs=[
            pl.BlockSpec((tm, d), lambda i, j: (i, 0)),
            pl.BlockSpec((1, 1, d), lambda i, j: (row(i), 0, 1)),
            pl.BlockSpec((1, 1, d), lambda i, j: (row(i), 0, 0)),
            pl.BlockSpec((1, d), lambda i, j: (0, 0)),
            pl.BlockSpec((d, SEG), lambda i, j: (0, j)),
            tab_spec, tab_spec, tab_spec, tab_spec,
        ],
        out_specs=pl.BlockSpec((tm, SEG), lambda i, j: (i, jnp.where(j >= 11, j - 1, j))),
        out_shape=jax.ShapeDtypeStruct((n_tok, 12 * SEG), BF16),
        scratch_shapes=[pltpu.VMEM((tm, d), BF16), pltpu.VMEM((tm, SEG), F32)],
        compiler_params=pltpu.CompilerParams(
            dimension_semantics=("parallel", "arbitrary"), vmem_limit_bytes=VMEM_LIMIT_BYTES),
        name="in_proj",
    )(x2d, mod, mod, norm_g.reshape(1, d), w_in, *tabs)


def _two_head_attention(q2, k_parts, v_parts, bias_parts):
    lane = lax.broadcasted_iota(jnp.int32, q2.shape, 1)
    first = lane < HEAD_DIM
    nt = (((1,), (1,)), ((), ()))
    out = None
    for hh in range(2):
        mine = first if hh == 0 else jnp.logical_not(first)
        qm = jnp.where(mine, q2, jnp.zeros_like(q2))
        s = []
        for kp, bp in zip(k_parts, bias_parts):
            sp = lax.dot_general(qm, kp, nt, preferred_element_type=F32)
            if bp is not None:
                sp = sp + bp[hh]
            s.append(sp)
        m = functools.reduce(jnp.maximum, [sp.max(axis=-1, keepdims=True) for sp in s])
        e = [jnp.exp(sp - m) for sp in s]
        denom = functools.reduce(jnp.add, [ep.sum(axis=-1, keepdims=True) for ep in e])
        o = functools.reduce(jnp.add, [jnp.dot(ep.astype(BF16), vp, preferred_element_type=F32)
                                       for ep, vp in zip(e, v_parts)])
        o = o / denom
        out = o if hh == 0 else jnp.where(first, out, o)
    return out


def _att_kernel(q_ref, k_ref, v_ref, kc_ref, vc_ref, b_ref, o_ref, *, rows):
    rb = pl.program_id(2)
    kc = kc_ref[...]
    vc = vc_ref[...]
    pair_q = 2 * GRID_W
    for pp in range(ATT_ROWS_PER_STEP // 2):
        r0 = rb * ATT_ROWS_PER_STEP + 2 * pp
        ks = jnp.clip(r0 - NA_KH // 2, 0, rows - KEY_ROWS)
        start = pl.multiple_of(ks * GRID_W, LANES)
        kw = k_ref[pl.ds(start, KEY_ROWS * GRID_W), :]
        vw = v_ref[pl.ds(start, KEY_ROWS * GRID_W), :]
        cls = jnp.where(r0 == 0, 0, jnp.where(r0 == 2, 1, jnp.where(
            r0 == rows - 4, 3, jnp.where(r0 == rows - 2, 4, 2))))
        q2 = q_ref[pp * pair_q:(pp + 1) * pair_q, :]
        out = _two_head_attention(q2, [kw, kc], [vw, vc], [b_ref.at[cls], None])
        o_ref[pp * pair_q:(pp + 1) * pair_q, :] = out.astype(BF16)


def _att_call(p_lat, p_ctx, bias, batch, seq_len, ctx_len):
    n_tok = p_lat.shape[0]
    rows = seq_len // GRID_W
    heads2 = (2 * SEG) // LANES
    rb_per_seq = rows // ATT_ROWS_PER_STEP
    tq = ATT_ROWS_PER_STEP * GRID_W
    qb, kb, vb = (2 * SEG) // LANES, (4 * SEG) // LANES, (6 * SEG) // LANES
    return pl.pallas_call(
        functools.partial(_att_kernel, rows=rows),
        grid=(batch, heads2, rb_per_seq),
        in_specs=[
            pl.BlockSpec((tq, LANES), lambda b, h, r: (b * rb_per_seq + r, qb + h)),
            pl.BlockSpec((seq_len, LANES), lambda b, h, r: (b, kb + h)),
            pl.BlockSpec((seq_len, LANES), lambda b, h, r: (b, vb + h)),
            pl.BlockSpec((ctx_len, LANES), lambda b, h, r: (b, kb + h)),
            pl.BlockSpec((ctx_len, LANES), lambda b, h, r: (b, vb + h)),
            pl.BlockSpec((5, 2, 2 * GRID_W, KEY_ROWS * GRID_W), lambda b, h, r: (0, h, 0, 0)),
        ],
        out_specs=pl.BlockSpec((tq, LANES), lambda b, h, r: (b * rb_per_seq + r, h)),
        out_shape=jax.ShapeDtypeStruct((n_tok, 2 * SEG), BF16),
        compiler_params=pltpu.CompilerParams(
            dimension_semantics=("parallel", "parallel", "arbitrary"),
            vmem_limit_bytes=VMEM_LIMIT_BYTES),
        name="nbr_attention",
    )(p_lat, p_lat, p_lat, p_ctx, p_ctx, bias)


def _ctx_att_kernel(q_ref, k_ref, v_ref, o_ref):
    out = _two_head_attention(q_ref[...], [k_ref[...]], [v_ref[...]], [None])
    o_ref[...] = out.astype(BF16)


def _ctx_att_call(p_ctx, batch, ctx_len):
    n_tok = p_ctx.shape[0]
    heads2 = (2 * SEG) // LANES
    qb, kb, vb = (2 * SEG) // LANES, (4 * SEG) // LANES, (6 * SEG) // LANES
    return pl.pallas_call(
        _ctx_att_kernel,
        grid=(batch, heads2),
        in_specs=[
            pl.BlockSpec((ctx_len, LANES), lambda b, h: (b, qb + h)),
            pl.BlockSpec((ctx_len, LANES), lambda b, h: (b, kb + h)),
            pl.BlockSpec((ctx_len, LANES), lambda b, h: (b, vb + h)),
        ],
        out_specs=pl.BlockSpec((ctx_len, LANES), lambda b, h: (b, h)),
        out_shape=jax.ShapeDtypeStruct((n_tok, 2 * SEG), BF16),
        compiler_params=pltpu.CompilerParams(
            dimension_semantics=("parallel", "parallel"), vmem_limit_bytes=VMEM_LIMIT_BYTES),
        name="ctx_attention",
    )(p_ctx, p_ctx, p_ctx)


def _attention_bias(na_rpb, rows):
    depth, heads = na_rpb.shape[:2]
    cq = np.arange(GRID_W)[:, None]
    ck = np.arange(GRID_W)[None, :]
    cs = np.clip(cq - NA_KW // 2, 0, GRID_W - NA_KW)
    col_ok = (ck >= cs) & (ck < cs + NA_KW)
    dc = np.clip(ck - cq + NA_KW - 1, 0, 2 * NA_KW - 2)
    b1 = jnp.where(col_ok, na_rpb[..., dc].astype(F32), MASK_VALUE)
    masked = jnp.full((depth, heads, 1, GRID_W, GRID_W), MASK_VALUE, F32)
    b1 = jnp.concatenate([b1, masked], axis=2)
    n_dr = 2 * NA_KH - 1
    idx = np.full((5, 2, KEY_ROWS), n_dr, np.int32)
    for c, r0 in enumerate((0, 2, 4, rows - 4, rows - 2)):
        ks = int(np.clip(r0 - NA_KH // 2, 0, rows - KEY_ROWS))
        for i in range(2):
            r = r0 + i
            rs = int(np.clip(r - NA_KH // 2, 0, rows - NA_KH))
            for jr in range(KEY_ROWS):
                kr = ks + jr
                if rs <= kr < rs + NA_KH:
                    idx[c, i, jr] = kr - r + NA_KH - 1
    t = b1[:, :, idx]
    t = jnp.transpose(t, (0, 2, 1, 3, 5, 4, 6))
    return t.reshape(depth, 5, heads, 2 * GRID_W, KEY_ROWS * GRID_W)


def _mix_kernel(x_ref, gate_ref, av_ref, avp_ref, avn_ref, ag_ref, at_ref, bg_ref,
                y_ref, yp_ref, yn_ref, cg_ref, wpool_ref, ps_ref, dw_ref, dwb_ref,
                lng_ref, lnb_ref, pw_ref, pwb_ref, wout_ref, fg_ref,
                o_ref, exa, exy, ycv, *, tm, seq_tiles, seq_len, final):
    s = pl.program_id(0) % seq_tiles
    first = s == 0
    last = s == seq_tiles - 1

    for ext, main, prev, nxt in ((exa, av_ref, avp_ref, avn_ref), (exy, y_ref, yp_ref, yn_ref)):
        ext[0:HALO, :] = jnp.where(first, 0.0, prev[...].astype(F32))
        ext[HALO:HALO + tm, :] = main[...].astype(F32)
        ext[HALO + tm:, :] = jnp.where(last, 0.0, nxt[...].astype(F32))

    group = exa.shape[1] // len(POOL_WINDOWS)
    t = s * tm + lax.broadcasted_iota(jnp.int32, (tm, group), 0)
    pooled = []
    for g, w in enumerate(POOL_WINDOWS):
        before, after = w // 2, w - w // 2 - 1
        cols = slice(g * group, (g + 1) * group)
        total = exa[HALO - before:HALO - before + tm, cols]
        for dlt in range(-before + 1, after + 1):
            total = total + exa[HALO + dlt:HALO + dlt + tm, cols]
        lo = jnp.clip(t - before, 0, seq_len - 1)
        hi = jnp.clip(t + after, 0, seq_len - 1)
        cnt = (hi - lo + 1).astype(F32)
        pooled.append(total / cnt - exa[HALO:HALO + tm, cols])
    pooled = jnp.concatenate(pooled, axis=1).astype(BF16)
    ya = jnp.dot(pooled, wpool_ref[...], preferred_element_type=F32)
    ya = ya * ps_ref[...] * ag_ref[...].astype(F32)

    pad = CONV_K // 2
    for c in range(tm // CONV_CHUNK):
        base = c * CONV_CHUNK + HALO - pad
        acc = exy[base:base + CONV_CHUNK, :] * dw_ref[0:1, :]
        for k in range(1, CONV_K):
            acc = acc + exy[base + k:base + k + CONV_CHUNK, :] * dw_ref[k:k + 1, :]
        ycv[c * CONV_CHUNK:(c + 1) * CONV_CHUNK, :] = acc + dwb_ref[...]

    yc = ycv[...]
    mu = jnp.mean(yc, axis=-1, keepdims=True)
    dev = yc - mu
    var = jnp.mean(dev * dev, axis=-1, keepdims=True)
    yn = dev * lax.rsqrt(var + LN_EPS) * lng_ref[...] + lnb_ref[...]
    cm = jnp.dot(_silu(yn).astype(BF16), pw_ref[...], preferred_element_type=F32) + pwb_ref[...]
    ycc = cm * cg_ref[...].astype(F32)

    yb = at_ref[...] * bg_ref[...]
    cat = jnp.concatenate([ya.astype(BF16), yb, ycc.astype(BF16)], axis=1)
    mixed = jnp.dot(cat, wout_ref[...], preferred_element_type=F32)
    xn = x_ref[...] + gate_ref[0] * mixed
    if final:
        ms = jnp.mean(xn * xn, axis=-1, keepdims=True)
        xn = xn * lax.rsqrt(ms + NORM_EPS) * fg_ref[...]
    o_ref[...] = xn


def _mix_call(x2d, mod, mod_row, p, att, wpool, pool_scale, conv_dw, conv_dw_b, ln_g, ln_b,
              conv_pw, conv_pw_b, w_out, final_g, seq_len, tm, final):
    n_tok, d = x2d.shape
    seq_tiles = seq_len // tm
    halo_blocks = tm // HALO
    n_halo = n_tok // HALO
    if mod_row is None:
        row = lambda i: i // seq_tiles
    else:
        row = lambda i: mod_row

    def main(colblk):
        return pl.BlockSpec((tm, SEG), lambda i: (i, colblk))

    def prev(colblk):
        return pl.BlockSpec((HALO, SEG), lambda i: (jnp.maximum(i * halo_blocks - 1, 0), colblk))

    def nxt(colblk):
        return pl.BlockSpec((HALO, SEG),
                            lambda i: (jnp.minimum((i + 1) * halo_blocks, n_halo - 1), colblk))

    def full(a):
        nd = a.ndim
        return pl.BlockSpec(a.shape, lambda i: (0,) * nd)

    small = [wpool, pool_scale.reshape(1, SEG), conv_dw, conv_dw_b.reshape(1, SEG),
             ln_g.reshape(1, SEG), ln_b.reshape(1, SEG), conv_pw, conv_pw_b.reshape(1, SEG),
             w_out, final_g.reshape(1, d)]
    kern = functools.partial(_mix_kernel, tm=tm, seq_tiles=seq_tiles, seq_len=seq_len, final=final)
    return pl.pallas_call(
        kern,
        grid=(n_tok // tm,),
        in_specs=[
            pl.BlockSpec((tm, d), lambda i: (i, 0)),
            pl.BlockSpec((1, 1, d), lambda i: (row(i), 0, 2)),
            main(0), prev(0), nxt(0),
            main(1),
            pl.BlockSpec((tm, 2 * SEG), lambda i: (i, 0)),
            pl.BlockSpec((tm, 2 * SEG), lambda i: (i, 4)),
            main(10), prev(10), nxt(10),
            main(11),
        ] + [full(a) for a in small],
        out_specs=pl.BlockSpec((tm, d), lambda i: (i, 0)),
        out_shape=jax.ShapeDtypeStruct((n_tok, d), F32),
        scratch_shapes=[pltpu.VMEM((tm + 2 * HALO, SEG), F32),
                        pltpu.VMEM((tm + 2 * HALO, SEG), F32),
                        pltpu.VMEM((tm, SEG), F32)],
        compiler_params=pltpu.CompilerParams(
            dimension_semantics=("parallel",), vmem_limit_bytes=VMEM_LIMIT_BYTES),
        name="mix_out",
    )(x2d, mod, p, p, p, p, att, p, p, p, p, p, *small)


def _rope_tables(seq_len):
    half = HEAD_DIM // 2
    freqs = ROPE_THETA ** (-jnp.arange(0, half, 2, dtype=F32) / half)
    t = jnp.arange(seq_len)
    ang_r = (t // GRID_W).astype(F32)[:, None] * freqs[None, :]
    ang_c = (t % GRID_W).astype(F32)[:, None] * freqs[None, :]
    cos = jnp.concatenate([jnp.cos(ang_r)] * 2 + [jnp.cos(ang_c)] * 2, axis=-1)
    sin = jnp.concatenate([-jnp.sin(ang_r), jnp.sin(ang_r), -jnp.sin(ang_c), jnp.sin(ang_c)], axis=-1)
    cos = jnp.tile(cos, (1, LANES // HEAD_DIM))
    sin = jnp.tile(sin, (1, LANES // HEAD_DIM))
    qs = HEAD_DIM ** -0.5
    return cos * qs, sin * qs, cos, sin


def kernel(x, c, ctx, c_ctx, w_mod, b_mod, norm_g, w_in, w_pool, pool_scale, na_rpb, conv_dw,
           conv_dw_b, conv_ln_g, conv_ln_b, conv_pw, conv_pw_b, w_out, final_norm_g):
    batch, seq_len, d = x.shape
    ctx_len = ctx.shape[1]
    depth = w_mod.shape[0]
    assert d == 4 * SEG and w_in.shape[2] == N_IN_STEPS * SEG
    assert seq_len % (ATT_ROWS_PER_STEP * GRID_W) == 0 and seq_len // GRID_W >= 12
    assert batch + 1 <= 8

    cstack = jnp.zeros((8, d), F32).at[:batch].set(c).at[batch].set(c_ctx)
    mod = _mod_call(cstack, w_mod, b_mod).reshape(depth, 8, 1, 3 * d)

    lat_tabs = _rope_tables(seq_len)
    qs = HEAD_DIM ** -0.5
    ones = jnp.ones((ctx_len, LANES), F32)
    ctx_tabs = (ones * qs, jnp.zeros_like(ones), ones, jnp.zeros_like(ones))
    bias = _attention_bias(na_rpb, seq_len // GRID_W)

    groups = len(POOL_WINDOWS)
    pg = w_pool.shape[-1]
    eye = jnp.eye(groups, dtype=F32)
    wpool_bd = (eye[None, :, None, :, None] * w_pool[:, :, :, None, :]).reshape(
        depth, groups * pg, groups * pg).astype(BF16)
    w_in_b = w_in.astype(BF16)
    w_out_b = w_out.astype(BF16)
    conv_pw_b16 = conv_pw.astype(BF16)

    tm_in = min(1024, seq_len)
    tm_mix = min(512, seq_len)
    xl = x.reshape(batch * seq_len, d)
    xc = ctx.reshape(batch * ctx_len, d)
    for i in range(depth):
        last = i == depth - 1
        mix_w = (wpool_bd[i], pool_scale[i], conv_dw[i], conv_dw_b[i], conv_ln_g[i], conv_ln_b[i],
                 conv_pw_b16[i], conv_pw_b[i], w_out_b[i], final_norm_g)
        p_ctx = _in_call(xc, mod[i], batch, norm_g[i], w_in_b[i], ctx_tabs, ctx_len, ctx_len)
        p_lat = _in_call(xl, mod[i], None, norm_g[i], w_in_b[i], lat_tabs, seq_len, tm_in)
        att = _att_call(p_lat, p_ctx, bias[i], batch, seq_len, ctx_len)
        xl = _mix_call(xl, mod[i], None, p_lat, att, *mix_w, seq_len, tm_mix, last)
        if not last:
            att_c = _ctx_att_call(p_ctx, batch, ctx_len)
            xc = _mix_call(xc, mod[i], batch, p_ctx, att_c, *mix_w, ctx_len, ctx_len, False)
    return xl.reshape(batch, seq_len, d)
```

```python
import functools

import numpy as np
import jax
import jax.numpy as jnp
from jax import lax
from jax.experimental import pallas as pl
from jax.experimental.pallas import tpu as pltpu

F32 = jnp.float32
BF16 = jnp.bfloat16

GRID_W = 64
POOL_WINDOWS = (2, 4, 8, 16)
HEAD_DIM = 64
NA_KH = 8
NA_KW = 16
ROPE_THETA = 10000.0
CONV_K = 31
NORM_EPS = 1e-6
LN_EPS = 1e-5

LANES = 128
VMEM_LIMIT_BYTES = 56 * 1024 * 1024

SEG = 512
SLABS = SEG // LANES
HALO = 16
MASK_VALUE = -1e30
KEY_ROWS = 10
N_EDGE_CLASSES = 5
ATT_ROWS_PER_STEP = 8
CONV_CHUNK = 64


def _sigmoid(v):
    return 1.0 / (1.0 + jnp.exp(-v))


def _silu(v):
    return v * _sigmoid(v)


def _mod_kernel(c_ref, w_ref, b_ref, o_ref):
    a = _silu(c_ref[...]).astype(BF16)
    o_ref[0] = jnp.dot(a, w_ref[0].astype(BF16), preferred_element_type=F32) + b_ref[0]


def _mod_call(cstack, w_mod, b_mod):
    depth, d, n = w_mod.shape
    tn = 768
    return pl.pallas_call(
        _mod_kernel,
        grid=(depth, n // tn),
        in_specs=[
            pl.BlockSpec((8, d), lambda l, j: (0, 0)),
            pl.BlockSpec((1, d, tn), lambda l, j: (l, 0, j)),
            pl.BlockSpec((1, 1, tn), lambda l, j: (l, 0, j)),
        ],
        out_specs=pl.BlockSpec((1, 8, tn), lambda l, j: (l, 0, j)),
        out_shape=jax.ShapeDtypeStruct((depth, 8, n), F32),
        compiler_params=pltpu.CompilerParams(
            dimension_semantics=("parallel", "parallel"), vmem_limit_bytes=VMEM_LIMIT_BYTES),
        name="adaln_mod",
    )(cstack, w_mod, b_mod.reshape(depth, 1, n))


_PLAIN, _SILU, _ROPE_Q, _ROPE_K, _GLU_A, _GLU_B = range(6)
IN_GROUP_KINDS = (_PLAIN, _SILU, _ROPE_Q, _ROPE_Q, _ROPE_K, _ROPE_K, _PLAIN, _PLAIN,
                  _SILU, _SILU, _GLU_A, _GLU_B, _SILU)
N_IN_GROUPS = len(IN_GROUP_KINDS)
N_OUT_GROUPS = N_IN_GROUPS - 1


def _rope(r, cos_ref, sin_ref):
    n = r.shape[1]
    reps = n // LANES
    lane = lax.broadcasted_iota(jnp.int32, r.shape, 1)
    up = pltpu.roll(r, n - 16, axis=1)
    dn = pltpu.roll(r, 16, axis=1)
    partner = jnp.where((lane % 32) < 16, up, dn)
    cos = jnp.tile(cos_ref[...], (1, reps))
    sin = jnp.tile(sin_ref[...], (1, reps))
    return r * cos + partner * sin


def _in_kernel(x_ref, scale_ref, shift_ref, g_ref, w_ref, cq_ref, sq_ref, ck_ref, sk_ref,
               o_ref, h_scr):
    x = x_ref[...]
    ms = jnp.mean(x * x, axis=-1, keepdims=True)
    gain = g_ref[...] * (1.0 + scale_ref[0])
    h_scr[...] = (x * lax.rsqrt(ms + NORM_EPS) * gain + shift_ref[0]).astype(BF16)

    def project(j):
        return jnp.dot(h_scr[...], w_ref[:, j * SEG:(j + 1) * SEG], preferred_element_type=F32)

    pending = project(0)
    glu_a = None
    col = 0
    for j, kind in enumerate(IN_GROUP_KINDS):
        r = pending
        if j + 1 < N_IN_GROUPS:
            pending = project(j + 1)
        if kind == _GLU_A:
            glu_a = r
            continue
        if kind == _PLAIN:
            out = r
        elif kind == _SILU:
            out = _silu(r)
        elif kind == _ROPE_Q:
            out = _rope(r, cq_ref, sq_ref)
        elif kind == _ROPE_K:
            out = _rope(r, ck_ref, sk_ref)
        else:
            out = glu_a * _sigmoid(r)
        o_ref[:, col * SEG:(col + 1) * SEG] = out.astype(BF16)
        col += 1


def _in_call(x2d, mod, mod_row, norm_g, w_in, tabs, seq_len, tm):
    n_tok, d = x2d.shape
    seq_tiles = seq_len // tm
    if mod_row is None:
        row = lambda i: i // seq_tiles
    else:
        row = lambda i: mod_row
    tab_spec = pl.BlockSpec((tm, LANES), lambda i: (i % seq_tiles, 0))
    return pl.pallas_call(
        _in_kernel,
        grid=(n_tok // tm,),
        in_specs=[
            pl.BlockSpec((tm, d), lambda i: (i, 0)),
            pl.BlockSpec((1, 1, d), lambda i: (row(i), 0, 1)),
            pl.BlockSpec((1, 1, d), lambda i: (row(i), 0, 0)),
            pl.BlockSpec((1, d), lambda i: (0, 0)),
            pl.BlockSpec(w_in.shape, lambda i: (0, 0), pipeline_mode=pl.Buffered(1)),
            tab_spec, tab_spec, tab_spec, tab_spec,
        ],
        out_specs=pl.BlockSpec((tm, N_OUT_GROUPS * SEG), lambda i: (i, 0)),
        out_shape=jax.ShapeDtypeStruct((n_tok, N_OUT_GROUPS * SEG), BF16),
        scratch_shapes=[pltpu.VMEM((tm, d), BF16)],
        compiler_params=pltpu.CompilerParams(
            dimension_semantics=("parallel",), vmem_limit_bytes=VMEM_LIMIT_BYTES),
        name="in_proj",
    )(x2d, mod, mod, norm_g.reshape(1, d), w_in, *tabs)


def _edge_class_rows(rows):
    n_dr = 2 * NA_KH - 1
    idx = np.full((N_EDGE_CLASSES, 2, KEY_ROWS), n_dr, np.int32)
    for c, r0 in enumerate((0, 2, 4, rows - 4, rows - 2)):
        ks = int(np.clip(r0 - NA_KH // 2, 0, rows - KEY_ROWS))
        for i in range(2):
            r = r0 + i
            rs = int(np.clip(r - NA_KH // 2, 0, rows - NA_KH))
            for jr in range(KEY_ROWS):
                kr = ks + jr
                if rs <= kr < rs + NA_KH:
                    idx[c, i, jr] = kr - r + NA_KH - 1
    return idx


def _att_kernel(q_ref, k_ref, v_ref, kc_ref, vc_ref, bd_ref, o_ref, tab, *, rows):
    rb = pl.program_id(2)
    pair_q = 2 * GRID_W
    n_pairs = ATT_ROWS_PER_STEP // 2
    nt = (((1,), (1,)), ((), ()))

    @pl.when(rb == 0)
    def _():
        idx = _edge_class_rows(rows)
        low = lax.broadcasted_iota(jnp.int32, (GRID_W, LANES), 1) < GRID_W
        for c in range(N_EDGE_CLASSES):
            for hh in range(2):
                for i in range(2):
                    r_lo = (2 * hh + i) * GRID_W
                    for m in range(KEY_ROWS // 2):
                        a, b = int(idx[c, i, 2 * m]), int(idx[c, i, 2 * m + 1])
                        tab[c, r_lo:r_lo + GRID_W, m * LANES:(m + 1) * LANES] = jnp.where(
                            low, bd_ref[hh, a], bd_ref[hh, b])

    kc = kc_ref[...]
    vc = vc_ref[...]
    first = lax.broadcasted_iota(jnp.int32, (pair_q, LANES), 1) < HEAD_DIM

    def scores(pp):
        r0 = rb * ATT_ROWS_PER_STEP + 2 * pp
        ks = jnp.clip(r0 - NA_KH // 2, 0, rows - KEY_ROWS)
        start = pl.multiple_of(ks * GRID_W, LANES)
        cls = jnp.where(r0 == 0, 0, jnp.where(r0 == 2, 1, jnp.where(
            r0 == rows - 4, 3, jnp.where(r0 == rows - 2, 4, 2))))
        q2 = q_ref[pp * pair_q:(pp + 1) * pair_q, :]
        zero = jnp.zeros_like(q2)
        qs = jnp.concatenate([jnp.where(first, q2, zero), jnp.where(first, zero, q2)], axis=0)
        kw = k_ref[pl.ds(start, KEY_ROWS * GRID_W), :]
        s_loc = lax.dot_general(qs, kw, nt, preferred_element_type=F32) + tab[cls]
        s_ctx = lax.dot_general(qs, kc, nt, preferred_element_type=F32)
        return s_loc, s_ctx, start

    def softmax(s_loc, s_ctx):
        m = jnp.maximum(s_loc.max(axis=-1, keepdims=True), s_ctx.max(axis=-1, keepdims=True))
        e_loc = jnp.exp(s_loc - m)
        e_ctx = jnp.exp(s_ctx - m)
        denom = e_loc.sum(axis=-1, keepdims=True) + e_ctx.sum(axis=-1, keepdims=True)
        return e_loc.astype(BF16), e_ctx.astype(BF16), denom

    def weighted_values(pp, e_loc, e_ctx, denom, start):
        vw = v_ref[pl.ds(start, KEY_ROWS * GRID_W), :]
        o = (jnp.dot(e_loc, vw, preferred_element_type=F32)
             + jnp.dot(e_ctx, vc, preferred_element_type=F32)) / denom
        out = jnp.where(first, o[:pair_q], o[pair_q:])
        o_ref[pp * pair_q:(pp + 1) * pair_q, :] = out.astype(BF16)

    pending = scores(0)
    for pp in range(n_pairs):
        s_loc, s_ctx, start = pending
        if pp + 1 < n_pairs:
            pending = scores(pp + 1)
        weighted_values(pp, *softmax(s_loc, s_ctx), start)


def _att_call(p_lat, p_ctx, bias_blocks, batch, seq_len, ctx_len):
    n_tok = p_lat.shape[0]
    rows = seq_len // GRID_W
    heads2 = (2 * SEG) // LANES
    rb_per_seq = rows // ATT_ROWS_PER_STEP
    tq = ATT_ROWS_PER_STEP * GRID_W
    qb, kb, vb = (2 * SEG) // LANES, (4 * SEG) // LANES, (6 * SEG) // LANES
    n_dr = bias_blocks.shape[1]
    return pl.pallas_call(
        functools.partial(_att_kernel, rows=rows),
        grid=(batch, heads2, rb_per_seq),
        in_specs=[
            pl.BlockSpec((tq, LANES), lambda b, h, r: (b * rb_per_seq + r, qb + h)),
            pl.BlockSpec((seq_len, LANES), lambda b, h, r: (b, kb + h)),
            pl.BlockSpec((seq_len, LANES), lambda b, h, r: (b, vb + h)),
            pl.BlockSpec((ctx_len, LANES), lambda b, h, r: (b, kb + h)),
            pl.BlockSpec((ctx_len, LANES), lambda b, h, r: (b, vb + h)),
            pl.BlockSpec((2, n_dr, GRID_W, LANES), lambda b, h, r: (h, 0, 0, 0)),
        ],
        out_specs=pl.BlockSpec((tq, LANES), lambda b, h, r: (b * rb_per_seq + r, h)),
        out_shape=jax.ShapeDtypeStruct((n_tok, 2 * SEG), BF16),
        scratch_shapes=[pltpu.VMEM((N_EDGE_CLASSES, 4 * GRID_W, KEY_ROWS * GRID_W), F32)],
        compiler_params=pltpu.CompilerParams(
            dimension_semantics=("arbitrary", "arbitrary", "arbitrary"),
            vmem_limit_bytes=VMEM_LIMIT_BYTES),
        name="nbr_attention",
    )(p_lat, p_lat, p_lat, p_ctx, p_ctx, bias_blocks)


def _two_head_attention(q2, k2, v2):
    m_rows = q2.shape[0]
    first = lax.broadcasted_iota(jnp.int32, q2.shape, 1) < HEAD_DIM
    zero = jnp.zeros_like(q2)
    qs = jnp.concatenate([jnp.where(first, q2, zero), jnp.where(first, zero, q2)], axis=0)
    s = lax.dot_general(qs, k2, (((1,), (1,)), ((), ())), preferred_element_type=F32)
    e = jnp.exp(s - s.max(axis=-1, keepdims=True))
    o = jnp.dot(e.astype(BF16), v2, preferred_element_type=F32) / e.sum(axis=-1, keepdims=True)
    return jnp.where(first, o[:m_rows], o[m_rows:])


def _ctx_att_kernel(q_ref, k_ref, v_ref, o_ref):
    o_ref[...] = _two_head_attention(q_ref[...], k_ref[...], v_ref[...]).astype(BF16)


def _ctx_att_call(p_ctx, batch, ctx_len):
    n_tok = p_ctx.shape[0]
    heads2 = (2 * SEG) // LANES
    qb, kb, vb = (2 * SEG) // LANES, (4 * SEG) // LANES, (6 * SEG) // LANES
    return pl.pallas_call(
        _ctx_att_kernel,
        grid=(batch, heads2),
        in_specs=[
            pl.BlockSpec((ctx_len, LANES), lambda b, h: (b, qb + h)),
            pl.BlockSpec((ctx_len, LANES), lambda b, h: (b, kb + h)),
            pl.BlockSpec((ctx_len, LANES), lambda b, h: (b, vb + h)),
        ],
        out_specs=pl.BlockSpec((ctx_len, LANES), lambda b, h: (b, h)),
        out_shape=jax.ShapeDtypeStruct((n_tok, 2 * SEG), BF16),
        compiler_params=pltpu.CompilerParams(
            dimension_semantics=("parallel", "parallel"), vmem_limit_bytes=VMEM_LIMIT_BYTES),
        name="ctx_attention",
    )(p_ctx, p_ctx, p_ctx)


def _bias_blocks(na_rpb):
    depth, heads = na_rpb.shape[:2]
    cq = np.arange(GRID_W)[:, None]
    ck = np.arange(GRID_W)[None, :]
    cs = np.clip(cq - NA_KW // 2, 0, GRID_W - NA_KW)
    col_ok = (ck >= cs) & (ck < cs + NA_KW)
    dc = np.clip(ck - cq + NA_KW - 1, 0, 2 * NA_KW - 2)
    b1 = jnp.where(col_ok, na_rpb[..., dc].astype(F32), MASK_VALUE)
    masked = jnp.full((depth, heads, 1, GRID_W, GRID_W), MASK_VALUE, F32)
    b1 = jnp.concatenate([b1, masked], axis=2)
    return jnp.concatenate([b1, b1], axis=-1)


def _mix_kernel(x_ref, gate_ref, av_ref, avp_ref, avn_ref, ag_ref, at_ref, bg_ref,
                y_ref, yp_ref, yn_ref, cg_ref, wpool_ref, ps_ref, dw_ref, dwb_ref,
                lng_ref, lnb_ref, pw_ref, pwb_ref, wout_ref, fg_ref,
                o_ref, exa, exy, pooled_scr, ycv, *, tm, seq_tiles, seq_len, final):
    s = pl.program_id(0) % seq_tiles
    first = s == 0
    last = s == seq_tiles - 1

    for ext, main, prev, nxt in ((exa, av_ref, avp_ref, avn_ref), (exy, y_ref, yp_ref, yn_ref)):
        for g in range(SLABS):
            cols = slice(g * LANES, (g + 1) * LANES)
            ext[g, 0:HALO, :] = jnp.where(first, 0.0, prev[:, cols].astype(F32))
            ext[g, HALO:HALO + tm, :] = main[:, cols].astype(F32)
            ext[g, HALO + tm:, :] = jnp.where(last, 0.0, nxt[:, cols].astype(F32))

    t = s * tm + lax.broadcasted_iota(jnp.int32, (tm, LANES), 0)
    for g, w in enumerate(POOL_WINDOWS):
        before, after = w // 2, w - w // 2 - 1
        total = exa[g, HALO - before:HALO - before + tm, :]
        for dlt in range(-before + 1, after + 1):
            total = total + exa[g, HALO + dlt:HALO + dlt + tm, :]
        lo = jnp.clip(t - before, 0, seq_len - 1)
        hi = jnp.clip(t + after, 0, seq_len - 1)
        cnt = (hi - lo + 1).astype(F32)
        pooled_scr[:, g * LANES:(g + 1) * LANES] = (
            total / cnt - exa[g, HALO:HALO + tm, :]).astype(BF16)
    ya = jnp.dot(pooled_scr[...], wpool_ref[...], preferred_element_type=F32)
    ya = ya * ps_ref[...] * ag_ref[...].astype(F32)

    pad = CONV_K // 2
    for c in range(tm // CONV_CHUNK):
        base = c * CONV_CHUNK + HALO - pad
        for g in range(SLABS):
            cols = slice(g * LANES, (g + 1) * LANES)
            acc = exy[g, base:base + CONV_CHUNK, :] * dw_ref[0:1, cols]
            for k in range(1, CONV_K):
                acc = acc + exy[g, base + k:base + k + CONV_CHUNK, :] * dw_ref[k:k + 1, cols]
            ycv[c * CONV_CHUNK:(c + 1) * CONV_CHUNK, cols] = acc + dwb_ref[:, cols]

    yc = ycv[...]
    mu = jnp.mean(yc, axis=-1, keepdims=True)
    dev = yc - mu
    var = jnp.mean(dev * dev, axis=-1, keepdims=True)
    yn = dev * lax.rsqrt(var + LN_EPS) * lng_ref[...] + lnb_ref[...]
    cm = jnp.dot(_silu(yn).astype(BF16), pw_ref[...], preferred_element_type=F32) + pwb_ref[...]
    ycc = cm * cg_ref[...].astype(F32)

    yb = at_ref[...] * bg_ref[...]
    cat = jnp.concatenate([ya.astype(BF16), yb, ycc.astype(BF16)], axis=1)
    mixed = jnp.dot(cat, wout_ref[...], preferred_element_type=F32)
    xn = x_ref[...] + gate_ref[0] * mixed
    if final:
        ms = jnp.mean(xn * xn, axis=-1, keepdims=True)
        xn = xn * lax.rsqrt(ms + NORM_EPS) * fg_ref[...]
    o_ref[...] = xn


def _mix_call(x2d, mod, mod_row, p, att, wpool, pool_scale, conv_dw, conv_dw_b, ln_g, ln_b,
              conv_pw, conv_pw_b, w_out, final_g, seq_len, tm, final):
    n_tok, d = x2d.shape
    seq_tiles = seq_len // tm
    halo_blocks = tm // HALO
    n_halo = n_tok // HALO
    if mod_row is None:
        row = lambda i: i // seq_tiles
    else:
        row = lambda i: mod_row

    def main(colblk):
        return pl.BlockSpec((tm, SEG), lambda i: (i, colblk))

    def prev(colblk):
        return pl.BlockSpec((HALO, SEG), lambda i: (jnp.maximum(i * halo_blocks - 1, 0), colblk))

    def nxt(colblk):
        return pl.BlockSpec((HALO, SEG),
                            lambda i: (jnp.minimum((i + 1) * halo_blocks, n_halo - 1), colblk))

    def full(a):
        nd = a.ndim
        return pl.BlockSpec(a.shape, lambda i: (0,) * nd, pipeline_mode=pl.Buffered(1))

    small = [wpool, pool_scale.reshape(1, SEG), conv_dw, conv_dw_b.reshape(1, SEG),
             ln_g.reshape(1, SEG), ln_b.reshape(1, SEG), conv_pw, conv_pw_b.reshape(1, SEG),
             w_out, final_g.reshape(1, d)]
    kern = functools.partial(_mix_kernel, tm=tm, seq_tiles=seq_tiles, seq_len=seq_len, final=final)
    return pl.pallas_call(
        kern,
        grid=(n_tok // tm,),
        in_specs=[
            pl.BlockSpec((tm, d), lambda i: (i, 0)),
            pl.BlockSpec((1, 1, d), lambda i: (row(i), 0, 2)),
            main(0), prev(0), nxt(0),
            main(1),
            pl.BlockSpec((tm, 2 * SEG), lambda i: (i, 0)),
            pl.BlockSpec((tm, 2 * SEG), lambda i: (i, 4)),
            main(10), prev(10), nxt(10),
            main(11),
        ] + [full(a) for a in small],
        out_specs=pl.BlockSpec((tm, d), lambda i: (i, 0)),
        out_shape=jax.ShapeDtypeStruct((n_tok, d), F32),
        scratch_shapes=[pltpu.VMEM((SLABS, tm + 2 * HALO, LANES), F32),
                        pltpu.VMEM((SLABS, tm + 2 * HALO, LANES), F32),
                        pltpu.VMEM((tm, SEG), BF16),
                        pltpu.VMEM((tm, SEG), F32)],
        compiler_params=pltpu.CompilerParams(
            dimension_semantics=("parallel",), vmem_limit_bytes=VMEM_LIMIT_BYTES),
        name="mix_out",
    )(x2d, mod, p, p, p, p, att, p, p, p, p, p, *small)


def _rope_tables(seq_len):
    half = HEAD_DIM // 2
    freqs = ROPE_THETA ** (-jnp.arange(0, half, 2, dtype=F32) / half)
    t = jnp.arange(seq_len)
    ang_r = (t // GRID_W).astype(F32)[:, None] * freqs[None, :]
    ang_c = (t % GRID_W).astype(F32)[:, None] * freqs[None, :]
    cos = jnp.concatenate([jnp.cos(ang_r)] * 2 + [jnp.cos(ang_c)] * 2, axis=-1)
    sin = jnp.concatenate([-jnp.sin(ang_r), jnp.sin(ang_r), -jnp.sin(ang_c), jnp.sin(ang_c)], axis=-1)
    cos = jnp.tile(cos, (1, LANES // HEAD_DIM))
    sin = jnp.tile(sin, (1, LANES // HEAD_DIM))
    qs = HEAD_DIM ** -0.5
    return cos * qs, sin * qs, cos, sin


def kernel(x, c, ctx, c_ctx, w_mod, b_mod, norm_g, w_in, w_pool, pool_scale, na_rpb, conv_dw,
           conv_dw_b, conv_ln_g, conv_ln_b, conv_pw, conv_pw_b, w_out, final_norm_g):
    batch, seq_len, d = x.shape
    ctx_len = ctx.shape[1]
    depth = w_mod.shape[0]
    assert d == 4 * SEG and w_in.shape[2] == N_IN_GROUPS * SEG
    assert seq_len % (ATT_ROWS_PER_STEP * GRID_W) == 0 and seq_len // GRID_W >= 12
    assert batch + 1 <= 8 and w_pool.shape[-1] == LANES

    cstack = jnp.zeros((8, d), F32).at[:batch].set(c).at[batch].set(c_ctx)
    mod = _mod_call(cstack, w_mod, b_mod).reshape(depth, 8, 1, 3 * d)

    lat_tabs = _rope_tables(seq_len)
    qs = HEAD_DIM ** -0.5
    ones = jnp.ones((ctx_len, LANES), F32)
    ctx_tabs = (ones * qs, jnp.zeros_like(ones), ones, jnp.zeros_like(ones))
    bias_blocks = _bias_blocks(na_rpb)

    groups = len(POOL_WINDOWS)
    pg = w_pool.shape[-1]
    eye = jnp.eye(groups, dtype=F32)
    wpool_bd = (eye[None, :, None, :, None] * w_pool[:, :, :, None, :]).reshape(
        depth, groups * pg, groups * pg).astype(BF16)
    w_in_b = w_in.astype(BF16)
    w_out_b = w_out.astype(BF16)
    conv_pw_b16 = conv_pw.astype(BF16)

    tm_in = min(256, seq_len)
    tm_mix = min(512, seq_len)
    xl = x.reshape(batch * seq_len, d)
    xc = ctx.reshape(batch * ctx_len, d)
    for i in range(depth):
        last = i == depth - 1
        mix_w = (wpool_bd[i], pool_scale[i], conv_dw[i], conv_dw_b[i], conv_ln_g[i], conv_ln_b[i],
                 conv_pw_b16[i], conv_pw_b[i], w_out_b[i], final_norm_g)
        p_ctx = _in_call(xc, mod[i], batch, norm_g[i], w_in_b[i], ctx_tabs, ctx_len,
                         min(tm_in, ctx_len))
        p_lat = _in_call(xl, mod[i], None, norm_g[i], w_in_b[i], lat_tabs, seq_len, tm_in)
        att = _att_call(p_lat, p_ctx, bias_blocks[i], batch, seq_len, ctx_len)
        xl = _mix_call(xl, mod[i], None, p_lat, att, *mix_w, seq_len, tm_mix, last)
        if not last:
            att_c = _ctx_att_call(p_ctx, batch, ctx_len)
            xc = _mix_call(xc, mod[i], batch, p_ctx, att_c, *mix_w, ctx_len, ctx_len, False)
    return xl.reshape(batch, seq_len, d)
```

```python
import functools

import numpy as np
import jax
import jax.numpy as jnp
from jax import lax
from jax.experimental import pallas as pl
from jax.experimental.pallas import tpu as pltpu

F32 = jnp.float32
BF16 = jnp.bfloat16

GRID_W = 64
POOL_WINDOWS = (2, 4, 8, 16)
HEAD_DIM = 64
NA_KH = 8
NA_KW = 16
ROPE_THETA = 10000.0
CONV_K = 31
NORM_EPS = 1e-6
LN_EPS = 1e-5

LANES = 128
VMEM_LIMIT_BYTES = 56 * 1024 * 1024

SEG = 512
SLABS = SEG // LANES
HALO = 16
MASK_VALUE = -1e30
KEY_ROWS = 10
N_EDGE_CLASSES = 5
ATT_ROWS_PER_STEP = 8
CONV_CHUNK = 32


def _sigmoid(v):
    return 1.0 / (1.0 + jnp.exp(-v))


def _silu(v):
    return v * _sigmoid(v)


def _layer_spec(a, layer, grid_rank):
    zeros = (0,) * (a.ndim - 1)
    if grid_rank == 1:
        index_map = lambda i: (layer,) + zeros
    else:
        index_map = lambda i, j: (layer,) + zeros
    return pl.BlockSpec((None,) + a.shape[1:], index_map, pipeline_mode=pl.Buffered(1))


def _mod_kernel(c_ref, w_ref, b_ref, o_ref):
    a = _silu(c_ref[...]).astype(BF16)
    o_ref[0] = jnp.dot(a, w_ref[0].astype(BF16), preferred_element_type=F32) + b_ref[0]


def _mod_call(cstack, w_mod, b_mod):
    depth, d, n = w_mod.shape
    tn = 768
    return pl.pallas_call(
        _mod_kernel,
        grid=(depth, n // tn),
        in_specs=[
            pl.BlockSpec((8, d), lambda l, j: (0, 0)),
            pl.BlockSpec((1, d, tn), lambda l, j: (l, 0, j)),
            pl.BlockSpec((1, 1, tn), lambda l, j: (l, 0, j)),
        ],
        out_specs=pl.BlockSpec((1, 8, tn), lambda l, j: (l, 0, j)),
        out_shape=jax.ShapeDtypeStruct((depth, 8, n), F32),
        compiler_params=pltpu.CompilerParams(
            dimension_semantics=("parallel", "parallel"), vmem_limit_bytes=VMEM_LIMIT_BYTES),
        name="adaln_mod",
    )(cstack, w_mod, b_mod.reshape(depth, 1, n))


_POOL, _SILU, _ROPE_Q, _ROPE_K, _PLAIN, _GLU_A, _GLU_B = range(7)
IN_GROUP_KINDS = (_POOL, _SILU, _ROPE_Q, _ROPE_Q, _ROPE_K, _ROPE_K, _PLAIN, _PLAIN,
                  _SILU, _SILU, _GLU_A, _GLU_B, _SILU)
N_IN_GROUPS = len(IN_GROUP_KINDS)
N_OUT_GROUPS = N_IN_GROUPS - 1
IN_GROUP_ORDER = (10, 11) + tuple(range(10)) + (12,)
NEEDS_HALO = (_POOL, _GLU_A, _GLU_B)


def _rope(r, cos_ref, sin_ref):
    n = r.shape[1]
    reps = n // LANES
    lane = lax.broadcasted_iota(jnp.int32, r.shape, 1)
    up = pltpu.roll(r, n - 16, axis=1)
    dn = pltpu.roll(r, 16, axis=1)
    partner = jnp.where((lane % 32) < 16, up, dn)
    cos = jnp.tile(cos_ref[...], (1, reps))
    sin = jnp.tile(sin_ref[...], (1, reps))
    return r * cos + partner * sin


def _in_kernel(x_ref, xp_ref, xn_ref, scale_ref, shift_ref, g_ref, w_ref,
               cq_ref, sq_ref, ck_ref, sk_ref, dw_ref, dwb_ref, lng_ref, lnb_ref,
               o_ref, h_scr, exa, exy, ycv, *, tm, seq_tiles, seq_len):
    s = pl.program_id(0) % seq_tiles
    first = s == 0
    last = s == seq_tiles - 1
    ext_rows = tm + 2 * HALO

    gain = g_ref[...] * (1.0 + scale_ref[...])
    shift = shift_ref[...]

    def normed(xv):
        ms = jnp.mean(xv * xv, axis=-1, keepdims=True)
        return (xv * lax.rsqrt(ms + NORM_EPS) * gain + shift).astype(BF16)

    h_scr[0:HALO, :] = normed(xp_ref[...])
    h_scr[HALO:HALO + tm, :] = normed(x_ref[...])
    h_scr[HALO + tm:, :] = normed(xn_ref[...])

    rid = lax.broadcasted_iota(jnp.int32, (ext_rows, LANES), 0)
    inside = (rid >= jnp.where(first, HALO, 0)) & (rid < jnp.where(last, HALO + tm, ext_rows))

    def project(j):
        cols = slice(j * SEG, (j + 1) * SEG)
        if IN_GROUP_KINDS[j] in NEEDS_HALO:
            return jnp.dot(h_scr[...], w_ref[:, cols], preferred_element_type=F32)
        return jnp.dot(h_scr[HALO:HALO + tm, :], w_ref[:, cols], preferred_element_type=F32)

    def pool_epilogue(r):
        for g in range(SLABS):
            exa[g] = jnp.where(inside, r[:, g * LANES:(g + 1) * LANES], 0.0)
        t = s * tm + lax.broadcasted_iota(jnp.int32, (tm, LANES), 0)
        for g, w in enumerate(POOL_WINDOWS):
            before, after = w // 2, w - w // 2 - 1
            total = exa[g, HALO - before:HALO - before + tm, :]
            for dlt in range(-before + 1, after + 1):
                total = total + exa[g, HALO + dlt:HALO + dlt + tm, :]
            lo = jnp.clip(t - before, 0, seq_len - 1)
            hi = jnp.clip(t + after, 0, seq_len - 1)
            cnt = (hi - lo + 1).astype(F32)
            o_ref[:, g * LANES:(g + 1) * LANES] = (
                total / cnt - exa[g, HALO:HALO + tm, :]).astype(BF16)

    def glu_epilogue(r):
        for g in range(SLABS):
            cols = slice(g * LANES, (g + 1) * LANES)
            exy[g] = jnp.where(inside, exy[g] * _sigmoid(r[:, cols]), 0.0)

    def conv_piece(c, g, anchor):
        base = c * CONV_CHUNK + HALO - CONV_K // 2
        cols = slice(g * LANES, (g + 1) * LANES)
        acc = anchor + dwb_ref[:, cols]
        for k in range(CONV_K):
            acc = acc + exy[g, base + k:base + k + CONV_CHUNK, :] * dw_ref[k:k + 1, cols]
        ycv[c * CONV_CHUNK:(c + 1) * CONV_CHUNK, cols] = acc

    def exact_zero_from(r):
        bits = pltpu.bitcast(r[0:CONV_CHUNK, 0:LANES], jnp.uint32)
        half = jnp.uint32(16)
        gone = lax.shift_right_logical(lax.shift_right_logical(bits, half), half)
        return pltpu.bitcast(gone, F32)

    def norm_swish():
        yc = ycv[...]
        mu = jnp.mean(yc, axis=-1, keepdims=True)
        dev = yc - mu
        var = jnp.mean(dev * dev, axis=-1, keepdims=True)
        yn = dev * lax.rsqrt(var + LN_EPS) * lng_ref[...] + lnb_ref[...]
        o_ref[:, 10 * SEG:11 * SEG] = _silu(yn).astype(BF16)

    pieces = [(c, g) for c in range(tm // CONV_CHUNK) for g in range(SLABS)]
    pieces_per_group = -(-len(pieces) // (N_IN_GROUPS - 4))
    conv_done = False

    pending = project(IN_GROUP_ORDER[0])
    for n, j in enumerate(IN_GROUP_ORDER):
        r = pending
        if n + 1 < N_IN_GROUPS:
            pending = project(IN_GROUP_ORDER[n + 1])
        kind = IN_GROUP_KINDS[j]
        if kind == _GLU_A:
            for g in range(SLABS):
                exy[g] = r[:, g * LANES:(g + 1) * LANES]
        elif kind == _GLU_B:
            glu_epilogue(r)
        elif kind == _POOL:
            pool_epilogue(r)
        else:
            if kind == _PLAIN:
                out = r
            elif kind == _SILU:
                out = _silu(r)
            elif kind == _ROPE_Q:
                out = _rope(r, cq_ref, sq_ref)
            else:
                out = _rope(r, ck_ref, sk_ref)
            col = j if j < 10 else j - 1
            o_ref[:, col * SEG:(col + 1) * SEG] = out.astype(BF16)
        if n >= 2 and pieces:
            anchor = exact_zero_from(r)
            for c, g in pieces[:pieces_per_group]:
                conv_piece(c, g, anchor)
            pieces = pieces[pieces_per_group:]
        elif n >= 2 and not conv_done:
            norm_swish()
            conv_done = True
    assert conv_done


def _in_call(x2d, layer, mod, mod_row, norm_g, w_in, tabs, conv_dw, conv_dw_b, ln_g, ln_b,
             seq_len, tm):
    n_tok, d = x2d.shape
    seq_tiles = seq_len // tm
    halo_blocks = tm // HALO
    n_halo = n_tok // HALO
    if mod_row is None:
        row = lambda i: i // seq_tiles
    else:
        row = lambda i: mod_row
    tab_spec = pl.BlockSpec((tm, LANES), lambda i: (i % seq_tiles, 0))
    kern = functools.partial(_in_kernel, tm=tm, seq_tiles=seq_tiles, seq_len=seq_len)
    return pl.pallas_call(
        kern,
        grid=(n_tok // tm,),
        in_specs=[
            pl.BlockSpec((tm, d), lambda i: (i, 0)),
            pl.BlockSpec((HALO, d), lambda i: (jnp.maximum(i * halo_blocks - 1, 0), 0)),
            pl.BlockSpec((HALO, d), lambda i: (jnp.minimum((i + 1) * halo_blocks, n_halo - 1), 0)),
            pl.BlockSpec((None, None, 1, d), lambda i: (layer, row(i), 0, 1)),
            pl.BlockSpec((None, None, 1, d), lambda i: (layer, row(i), 0, 0)),
            _layer_spec(norm_g, layer, 1),
            _layer_spec(w_in, layer, 1),
            tab_spec, tab_spec, tab_spec, tab_spec,
            _layer_spec(conv_dw, layer, 1), _layer_spec(conv_dw_b, layer, 1),
            _layer_spec(ln_g, layer, 1), _layer_spec(ln_b, layer, 1),
        ],
        out_specs=pl.BlockSpec((tm, N_OUT_GROUPS * SEG), lambda i: (i, 0)),
        out_shape=jax.ShapeDtypeStruct((n_tok, N_OUT_GROUPS * SEG), BF16),
        scratch_shapes=[pltpu.VMEM((tm + 2 * HALO, d), BF16),
                        pltpu.VMEM((SLABS, tm + 2 * HALO, LANES), F32),
                        pltpu.VMEM((SLABS, tm + 2 * HALO, LANES), F32),
                        pltpu.VMEM((tm, SEG), F32)],
        compiler_params=pltpu.CompilerParams(
            dimension_semantics=("parallel",), vmem_limit_bytes=VMEM_LIMIT_BYTES),
        name="in_proj",
    )(x2d, x2d, x2d, mod, mod, norm_g, w_in, *tabs, conv_dw, conv_dw_b, ln_g, ln_b)


def _edge_class_rows(rows):
    n_dr = 2 * NA_KH - 1
    idx = np.full((N_EDGE_CLASSES, 2, KEY_ROWS), n_dr, np.int32)
    for c, r0 in enumerate((0, 2, 4, rows - 4, rows - 2)):
        ks = int(np.clip(r0 - NA_KH // 2, 0, rows - KEY_ROWS))
        for i in range(2):
            r = r0 + i
            rs = int(np.clip(r - NA_KH // 2, 0, rows - NA_KH))
            for jr in range(KEY_ROWS):
                kr = ks + jr
                if rs <= kr < rs + NA_KH:
                    idx[c, i, jr] = kr - r + NA_KH - 1
    return idx


def _att_kernel(q_ref, k_ref, v_ref, kc_ref, vc_ref, bd_ref, o_ref, tab, *, rows):
    rb = pl.program_id(2)
    pair_q = 2 * GRID_W
    n_pairs = ATT_ROWS_PER_STEP // 2
    nt = (((1,), (1,)), ((), ()))

    @pl.when(rb == 0)
    def _():
        idx = _edge_class_rows(rows)
        low = lax.broadcasted_iota(jnp.int32, (GRID_W, LANES), 1) < GRID_W
        for c in range(N_EDGE_CLASSES):
            for hh in range(2):
                for i in range(2):
                    r_lo = (2 * hh + i) * GRID_W
                    for m in range(KEY_ROWS // 2):
                        a, b = int(idx[c, i, 2 * m]), int(idx[c, i, 2 * m + 1])
                        tab[c, r_lo:r_lo + GRID_W, m * LANES:(m + 1) * LANES] = jnp.where(
                            low, bd_ref[hh, a], bd_ref[hh, b])

    kc = kc_ref[...]
    vc = vc_ref[...]
    first = lax.broadcasted_iota(jnp.int32, (pair_q, LANES), 1) < HEAD_DIM

    def scores(pp):
        r0 = rb * ATT_ROWS_PER_STEP + 2 * pp
        ks = jnp.clip(r0 - NA_KH // 2, 0, rows - KEY_ROWS)
        start = pl.multiple_of(ks * GRID_W, LANES)
        cls = jnp.where(r0 == 0, 0, jnp.where(r0 == 2, 1, jnp.where(
            r0 == rows - 4, 3, jnp.where(r0 == rows - 2, 4, 2))))
        q2 = q_ref[pp * pair_q:(pp + 1) * pair_q, :]
        zero = jnp.zeros_like(q2)
        qs = jnp.concatenate([jnp.where(first, q2, zero), jnp.where(first, zero, q2)], axis=0)
        kw = k_ref[pl.ds(start, KEY_ROWS * GRID_W), :]
        s_loc = lax.dot_general(qs, kw, nt, preferred_element_type=F32) + tab[cls]
        s_ctx = lax.dot_general(qs, kc, nt, preferred_element_type=F32)
        return s_loc, s_ctx, start

    def softmax(s_loc, s_ctx):
        m = jnp.maximum(s_loc.max(axis=-1, keepdims=True), s_ctx.max(axis=-1, keepdims=True))
        e_loc = jnp.exp(s_loc - m)
        e_ctx = jnp.exp(s_ctx - m)
        denom = e_loc.sum(axis=-1, keepdims=True) + e_ctx.sum(axis=-1, keepdims=True)
        return e_loc.astype(BF16), e_ctx.astype(BF16), denom

    def weighted_values(pp, e_loc, e_ctx, denom, start):
        vw = v_ref[pl.ds(start, KEY_ROWS * GRID_W), :]
        o = (jnp.dot(e_loc, vw, preferred_element_type=F32)
             + jnp.dot(e_ctx, vc, preferred_element_type=F32)) / denom
        out = jnp.where(first, o[:pair_q], o[pair_q:])
        o_ref[pp * pair_q:(pp + 1) * pair_q, :] = out.astype(BF16)

    pending = scores(0)
    for pp in range(n_pairs):
        s_loc, s_ctx, start = pending
        if pp + 1 < n_pairs:
            pending = scores(pp + 1)
        weighted_values(pp, *softmax(s_loc, s_ctx), start)


def _att_call(p_lat, p_ctx, layer, bias_blocks, batch, seq_len, ctx_len):
    n_tok = p_lat.shape[0]
    rows = seq_len // GRID_W
    heads2 = (2 * SEG) // LANES
    rb_per_seq = rows // ATT_ROWS_PER_STEP
    tq = ATT_ROWS_PER_STEP * GRID_W
    qb, kb, vb = (2 * SEG) // LANES, (4 * SEG) // LANES, (6 * SEG) // LANES
    n_dr = bias_blocks.shape[2]
    return pl.pallas_call(
        functools.partial(_att_kernel, rows=rows),
        grid=(batch, heads2, rb_per_seq),
        in_specs=[
            pl.BlockSpec((tq, LANES), lambda b, h, r: (b * rb_per_seq + r, qb + h)),
            pl.BlockSpec((seq_len, LANES), lambda b, h, r: (b, kb + h)),
            pl.BlockSpec((seq_len, LANES), lambda b, h, r: (b, vb + h)),
            pl.BlockSpec((ctx_len, LANES), lambda b, h, r: (b, kb + h)),
            pl.BlockSpec((ctx_len, LANES), lambda b, h, r: (b, vb + h)),
            pl.BlockSpec((None, 2, n_dr, GRID_W, LANES), lambda b, h, r: (layer, h, 0, 0, 0)),
        ],
        out_specs=pl.BlockSpec((tq, LANES), lambda b, h, r: (b * rb_per_seq + r, h)),
        out_shape=jax.ShapeDtypeStruct((n_tok, 2 * SEG), BF16),
        scratch_shapes=[pltpu.VMEM((N_EDGE_CLASSES, 4 * GRID_W, KEY_ROWS * GRID_W), F32)],
        compiler_params=pltpu.CompilerParams(
            dimension_semantics=("arbitrary", "arbitrary", "arbitrary"),
            vmem_limit_bytes=VMEM_LIMIT_BYTES),
        name="nbr_attention",
    )(p_lat, p_lat, p_lat, p_ctx, p_ctx, bias_blocks)


def _two_head_attention(q2, k2, v2):
    m_rows = q2.shape[0]
    first = lax.broadcasted_iota(jnp.int32, q2.shape, 1) < HEAD_DIM
    zero = jnp.zeros_like(q2)
    qs = jnp.concatenate([jnp.where(first, q2, zero), jnp.where(first, zero, q2)], axis=0)
    s = lax.dot_general(qs, k2, (((1,), (1,)), ((), ())), preferred_element_type=F32)
    e = jnp.exp(s - s.max(axis=-1, keepdims=True))
    o = jnp.dot(e.astype(BF16), v2, preferred_element_type=F32) / e.sum(axis=-1, keepdims=True)
    return jnp.where(first, o[:m_rows], o[m_rows:])


def _ctx_att_kernel(q_ref, k_ref, v_ref, o_ref):
    o_ref[...] = _two_head_attention(q_ref[...], k_ref[...], v_ref[...]).astype(BF16)


def _ctx_att_call(p_ctx, batch, ctx_len):
    n_tok = p_ctx.shape[0]
    heads2 = (2 * SEG) // LANES
    qb, kb, vb = (2 * SEG) // LANES, (4 * SEG) // LANES, (6 * SEG) // LANES
    return pl.pallas_call(
        _ctx_att_kernel,
        grid=(batch, heads2),
        in_specs=[
            pl.BlockSpec((ctx_len, LANES), lambda b, h: (b, qb + h)),
            pl.BlockSpec((ctx_len, LANES), lambda b, h: (b, kb + h)),
            pl.BlockSpec((ctx_len, LANES), lambda b, h: (b, vb + h)),
        ],
        out_specs=pl.BlockSpec((ctx_len, LANES), lambda b, h: (b, h)),
        out_shape=jax.ShapeDtypeStruct((n_tok, 2 * SEG), BF16),
        compiler_params=pltpu.CompilerParams(
            dimension_semantics=("parallel", "parallel"), vmem_limit_bytes=VMEM_LIMIT_BYTES),
        name="ctx_attention",
    )(p_ctx, p_ctx, p_ctx)


def _bias_blocks(na_rpb):
    depth, heads, n_dr, n_dc = na_rpb.shape
    cq = np.arange(GRID_W)[:, None]
    ck = np.arange(GRID_W)[None, :]
    cs = np.clip(cq - NA_KW // 2, 0, GRID_W - NA_KW)
    col_ok = (ck >= cs) & (ck < cs + NA_KW)
    dc = ck - cq + NA_KW - 1
    onehot = ((np.arange(n_dc)[:, None, None] == dc[None]) & col_ok[None]).astype(np.float32)
    b1 = jnp.einsum("lhrd,dqk->lhrqk", na_rpb.astype(F32), jnp.asarray(onehot),
                    precision=lax.Precision.HIGHEST)
    b1 = jnp.where(col_ok, b1, MASK_VALUE)
    masked = jnp.full((depth, heads, 1, GRID_W, GRID_W), MASK_VALUE, F32)
    b1 = jnp.concatenate([b1, masked], axis=2)
    return jnp.concatenate([b1, b1], axis=-1)


def _mix_kernel(x_ref, gate_ref, pooled_ref, ag_ref, at_ref, bg_ref, sw_ref, cg_ref,
                wpool_ref, ps_ref, pw_ref, pwb_ref, wout_ref, fg_ref, o_ref, *, final):
    ya = jnp.dot(pooled_ref[...], wpool_ref[...], preferred_element_type=F32)
    ya = ya * ps_ref[...] * ag_ref[...].astype(F32)
    cm = jnp.dot(sw_ref[...], pw_ref[...], preferred_element_type=F32) + pwb_ref[...]
    ycc = cm * cg_ref[...].astype(F32)
    yb = at_ref[...] * bg_ref[...]
    cat = jnp.concatenate([ya.astype(BF16), yb, ycc.astype(BF16)], axis=1)
    mixed = jnp.dot(cat, wout_ref[...], preferred_element_type=F32)
    xn = x_ref[...] + gate_ref[...] * mixed
    if final:
        ms = jnp.mean(xn * xn, axis=-1, keepdims=True)
        xn = xn * lax.rsqrt(ms + NORM_EPS) * fg_ref[...]
    o_ref[...] = xn


def _mix_call(x2d, layer, mod, mod_row, p, att, wpool, pool_scale, conv_pw, conv_pw_b, w_out,
              final_g, seq_len, tm, final):
    n_tok, d = x2d.shape
    seq_tiles = seq_len // tm
    if mod_row is None:
        row = lambda i: i // seq_tiles
    else:
        row = lambda i: mod_row

    def group(colblk):
        return pl.BlockSpec((tm, SEG), lambda i: (i, colblk))

    return pl.pallas_call(
        functools.partial(_mix_kernel, final=final),
        grid=(n_tok // tm,),
        in_specs=[
            pl.BlockSpec((tm, d), lambda i: (i, 0)),
            pl.BlockSpec((None, None, 1, d), lambda i: (layer, row(i), 0, 2)),
            group(0),
            group(1),
            pl.BlockSpec((tm, 2 * SEG), lambda i: (i, 0)),
            pl.BlockSpec((tm, 2 * SEG), lambda i: (i, 4)),
            group(10),
            group(11),
            _layer_spec(wpool, layer, 1), _layer_spec(pool_scale, layer, 1),
            _layer_spec(conv_pw, layer, 1), _layer_spec(conv_pw_b, layer, 1),
            _layer_spec(w_out, layer, 1),
            pl.BlockSpec(final_g.shape, lambda i: (0, 0), pipeline_mode=pl.Buffered(1)),
        ],
        out_specs=pl.BlockSpec((tm, d), lambda i: (i, 0)),
        out_shape=jax.ShapeDtypeStruct((n_tok, d), F32),
        compiler_params=pltpu.CompilerParams(
            dimension_semantics=("parallel",), vmem_limit_bytes=VMEM_LIMIT_BYTES),
        name="mix_out",
    )(x2d, mod, p, p, att, p, p, p, wpool, pool_scale, conv_pw, conv_pw_b, w_out, final_g)


def _rope_tables(seq_len):
    half = HEAD_DIM // 2
    freqs = ROPE_THETA ** (-jnp.arange(0, half, 2, dtype=F32) / half)
    t = jnp.arange(seq_len)
    ang_r = (t // GRID_W).astype(F32)[:, None] * freqs[None, :]
    ang_c = (t % GRID_W).astype(F32)[:, None] * freqs[None, :]
    cos = jnp.concatenate([jnp.cos(ang_r)] * 2 + [jnp.cos(ang_c)] * 2, axis=-1)
    sin = jnp.concatenate([-jnp.sin(ang_r), jnp.sin(ang_r), -jnp.sin(ang_c), jnp.sin(ang_c)], axis=-1)
    cos = jnp.tile(cos, (1, LANES // HEAD_DIM))
    sin = jnp.tile(sin, (1, LANES // HEAD_DIM))
    qs = HEAD_DIM ** -0.5
    return cos * qs, sin * qs, cos, sin


def kernel(x, c, ctx, c_ctx, w_mod, b_mod, norm_g, w_in, w_pool, pool_scale, na_rpb, conv_dw,
           conv_dw_b, conv_ln_g, conv_ln_b, conv_pw, conv_pw_b, w_out, final_norm_g):
    batch, seq_len, d = x.shape
    ctx_len = ctx.shape[1]
    depth = w_mod.shape[0]
    assert d == 4 * SEG and w_in.shape[2] == N_IN_GROUPS * SEG
    assert seq_len % (ATT_ROWS_PER_STEP * GRID_W) == 0 and seq_len // GRID_W >= 12
    assert batch + 1 <= 8 and w_pool.shape[-1] == LANES

    cstack = jnp.zeros((8, d), F32).at[:batch].set(c).at[batch].set(c_ctx)
    mod = _mod_call(cstack, w_mod, b_mod).reshape(depth, 8, 1, 3 * d)

    lat_tabs = _rope_tables(seq_len)
    qs = HEAD_DIM ** -0.5
    ones = jnp.ones((ctx_len, LANES), F32)
    ctx_tabs = (ones * qs, jnp.zeros_like(ones), ones, jnp.zeros_like(ones))
    bias_blocks = _bias_blocks(na_rpb)

    groups = len(POOL_WINDOWS)
    pg = w_pool.shape[-1]
    eye = jnp.eye(groups, dtype=F32)
    wpool_bd = (eye[None, :, None, :, None] * w_pool[:, :, :, None, :]).reshape(
        depth, groups * pg, groups * pg).astype(BF16)
    w_in_b = w_in.astype(BF16)
    w_out_b = w_out.astype(BF16)
    conv_pw_b16 = conv_pw.astype(BF16)
    row_vec = lambda a: a.reshape(depth, 1, a.shape[-1])
    in_w = (row_vec(norm_g), w_in_b)
    conv_w = (conv_dw, row_vec(conv_dw_b), row_vec(conv_ln_g), row_vec(conv_ln_b))
    mix_w = (wpool_bd, row_vec(pool_scale), conv_pw_b16, row_vec(conv_pw_b), w_out_b,
             final_norm_g.reshape(1, d))

    tm_in = min(256, seq_len)
    tm_mix = min(512, seq_len)
    xl = x.reshape(batch * seq_len, d)
    xc = ctx.reshape(batch * ctx_len, d)
    for i in range(depth):
        last = i == depth - 1
        p_ctx = _in_call(xc, i, mod, batch, *in_w, ctx_tabs, *conv_w, ctx_len, min(tm_in, ctx_len))
        p_lat = _in_call(xl, i, mod, None, *in_w, lat_tabs, *conv_w, seq_len, tm_in)
        att = _att_call(p_lat, p_ctx, i, bias_blocks, batch, seq_len, ctx_len)
        xl = _mix_call(xl, i, mod, None, p_lat, att, *mix_w, seq_len, tm_mix, last)
        if not last:
            att_c = _ctx_att_call(p_ctx, batch, ctx_len)
            xc = _mix_call(xc, i, mod, batch, p_ctx, att_c, *mix_w, ctx_len, ctx_len, False)
    return xl.reshape(batch, seq_len, d)
```

```python
import functools

import numpy as np
import jax
import jax.numpy as jnp
from jax import lax
from jax.experimental import pallas as pl
from jax.experimental.pallas import tpu as pltpu

F32 = jnp.float32
BF16 = jnp.bfloat16

GRID_W = 64
POOL_WINDOWS = (2, 4, 8, 16)
HEAD_DIM = 64
NA_KH = 8
NA_KW = 16
ROPE_THETA = 10000.0
CONV_K = 31
NORM_EPS = 1e-6
LN_EPS = 1e-5

LANES = 128
VMEM_LIMIT_BYTES = 56 * 1024 * 1024

SEG = 512
SLABS = SEG // LANES
HALO = 16
MASK_VALUE = -1e30
KEY_ROWS = 10
N_EDGE_CLASSES = 5
ATT_ROWS_PER_STEP = 16
LOG2E = 1.4426950408889634
Q_SCALE = HEAD_DIM ** -0.5 * LOG2E
CONV_CHUNK = 32


def _sigmoid(v):
    return 1.0 / (1.0 + jnp.exp(-v))


def _silu(v):
    return v * _sigmoid(v)


def _layer_spec(a, layer, grid_rank):
    zeros = (0,) * (a.ndim - 1)
    if grid_rank == 1:
        index_map = lambda i: (layer,) + zeros
    else:
        index_map = lambda i, j: (layer,) + zeros
    return pl.BlockSpec((None,) + a.shape[1:], index_map, pipeline_mode=pl.Buffered(1))


def _mod_kernel(c_ref, w_ref, b_ref, o_ref):
    a = _silu(c_ref[...]).astype(BF16)
    o_ref[0] = jnp.dot(a, w_ref[0].astype(BF16), preferred_element_type=F32) + b_ref[0]


def _mod_call(cstack, w_mod, b_mod):
    depth, d, n = w_mod.shape
    tn = 768
    return pl.pallas_call(
        _mod_kernel,
        grid=(depth, n // tn),
        in_specs=[
            pl.BlockSpec((8, d), lambda l, j: (0, 0)),
            pl.BlockSpec((1, d, tn), lambda l, j: (l, 0, j)),
            pl.BlockSpec((1, 1, tn), lambda l, j: (l, 0, j)),
        ],
        out_specs=pl.BlockSpec((1, 8, tn), lambda l, j: (l, 0, j)),
        out_shape=jax.ShapeDtypeStruct((depth, 8, n), F32),
        compiler_params=pltpu.CompilerParams(
            dimension_semantics=("parallel", "parallel"), vmem_limit_bytes=VMEM_LIMIT_BYTES),
        name="adaln_mod",
    )(cstack, w_mod, b_mod.reshape(depth, 1, n))


_POOL, _SILU, _ROPE_Q, _ROPE_K, _PLAIN, _GLU_A, _GLU_B = range(7)
IN_GROUP_KINDS = (_POOL, _SILU, _ROPE_Q, _ROPE_Q, _ROPE_K, _ROPE_K, _PLAIN, _PLAIN,
                  _SILU, _SILU, _GLU_A, _GLU_B, _SILU)
N_IN_GROUPS = len(IN_GROUP_KINDS)
N_OUT_GROUPS = N_IN_GROUPS - 1
IN_GROUP_ORDER = (10, 11) + tuple(range(10)) + (12,)
NEEDS_HALO = (_POOL, _GLU_A, _GLU_B)


def _rope(r, cos_ref, sin_ref):
    n = r.shape[1]
    reps = n // LANES
    lane = lax.broadcasted_iota(jnp.int32, r.shape, 1)
    up = pltpu.roll(r, n - 16, axis=1)
    dn = pltpu.roll(r, 16, axis=1)
    partner = jnp.where((lane % 32) < 16, up, dn)
    cos = jnp.tile(cos_ref[...], (1, reps))
    sin = jnp.tile(sin_ref[...], (1, reps))
    return r * cos + partner * sin


def _in_kernel(x_ref, xp_ref, xn_ref, scale_ref, shift_ref, g_ref, w_ref,
               cq_ref, sq_ref, ck_ref, sk_ref, dw_ref, dwb_ref, lng_ref, lnb_ref,
               o_ref, h_scr, exa, exy, ycv, *, tm, seq_tiles, seq_len):
    s = pl.program_id(0) % seq_tiles
    first = s == 0
    last = s == seq_tiles - 1
    ext_rows = tm + 2 * HALO

    gain = g_ref[...] * (1.0 + scale_ref[...])
    shift = shift_ref[...]

    def normed(xv):
        ms = jnp.mean(xv * xv, axis=-1, keepdims=True)
        return (xv * lax.rsqrt(ms + NORM_EPS) * gain + shift).astype(BF16)

    h_scr[0:HALO, :] = normed(xp_ref[...])
    h_scr[HALO:HALO + tm, :] = normed(x_ref[...])
    h_scr[HALO + tm:, :] = normed(xn_ref[...])

    rid = lax.broadcasted_iota(jnp.int32, (ext_rows, LANES), 0)
    inside = (rid >= jnp.where(first, HALO, 0)) & (rid < jnp.where(last, HALO + tm, ext_rows))

    def project(j):
        cols = slice(j * SEG, (j + 1) * SEG)
        if IN_GROUP_KINDS[j] in NEEDS_HALO:
            return jnp.dot(h_scr[...], w_ref[:, cols], preferred_element_type=F32)
        return jnp.dot(h_scr[HALO:HALO + tm, :], w_ref[:, cols], preferred_element_type=F32)

    def pool_epilogue(r):
        for g in range(SLABS):
            exa[g] = jnp.where(inside, r[:, g * LANES:(g + 1) * LANES], 0.0)
        t = s * tm + lax.broadcasted_iota(jnp.int32, (tm, LANES), 0)
        for g, w in enumerate(POOL_WINDOWS):
            before, after = w // 2, w - w // 2 - 1
            total = exa[g, HALO - before:HALO - before + tm, :]
            for dlt in range(-before + 1, after + 1):
                total = total + exa[g, HALO + dlt:HALO + dlt + tm, :]
            lo = jnp.clip(t - before, 0, seq_len - 1)
            hi = jnp.clip(t + after, 0, seq_len - 1)
            cnt = (hi - lo + 1).astype(F32)
            o_ref[:, g * LANES:(g + 1) * LANES] = (
                total / cnt - exa[g, HALO:HALO + tm, :]).astype(BF16)

    def glu_epilogue(r):
        for g in range(SLABS):
            cols = slice(g * LANES, (g + 1) * LANES)
            exy[g] = jnp.where(inside, exy[g] * _sigmoid(r[:, cols]), 0.0)

    def conv_piece(c, g, anchor):
        base = c * CONV_CHUNK + HALO - CONV_K // 2
        cols = slice(g * LANES, (g + 1) * LANES)
        acc = dwb_ref[:, cols]
        for k in range(CONV_K):
            tap = jnp.tile(dw_ref[k:k + 1, cols] + anchor, (CONV_CHUNK // 8, 1))
            acc = acc + exy[g, base + k:base + k + CONV_CHUNK, :] * tap
        ycv[c * CONV_CHUNK:(c + 1) * CONV_CHUNK, cols] = acc

    def exact_zero_from(r):
        bits = pltpu.bitcast(r[0:8, 0:LANES], jnp.uint32)
        half = jnp.uint32(16)
        gone = lax.shift_right_logical(lax.shift_right_logical(bits, half), half)
        return pltpu.bitcast(gone, F32)

    def norm_swish():
        yc = ycv[...]
        mu = jnp.mean(yc, axis=-1, keepdims=True)
        dev = yc - mu
        var = jnp.mean(dev * dev, axis=-1, keepdims=True)
        yn = dev * lax.rsqrt(var + LN_EPS) * lng_ref[...] + lnb_ref[...]
        o_ref[:, 10 * SEG:11 * SEG] = _silu(yn).astype(BF16)

    pieces = [(c, g) for c in range(tm // CONV_CHUNK) for g in range(SLABS)]
    pieces_per_group = -(-len(pieces) // (N_IN_GROUPS - 4))
    conv_done = False

    pending = project(IN_GROUP_ORDER[0])
    for n, j in enumerate(IN_GROUP_ORDER):
        r = pending
        if n + 1 < N_IN_GROUPS:
            pending = project(IN_GROUP_ORDER[n + 1])
        kind = IN_GROUP_KINDS[j]
        if kind == _GLU_A:
            for g in range(SLABS):
                exy[g] = r[:, g * LANES:(g + 1) * LANES]
        elif kind == _GLU_B:
            glu_epilogue(r)
        elif kind == _POOL:
            pool_epilogue(r)
        else:
            if kind == _PLAIN:
                out = r
            elif kind == _SILU:
                out = _silu(r)
            elif kind == _ROPE_Q:
                out = _rope(r, cq_ref, sq_ref)
            else:
                out = _rope(r, ck_ref, sk_ref)
            col = j if j < 10 else j - 1
            o_ref[:, col * SEG:(col + 1) * SEG] = out.astype(BF16)
        if n >= 2 and pieces:
            anchor = exact_zero_from(r)
            for c, g in pieces[:pieces_per_group]:
                conv_piece(c, g, anchor)
            pieces = pieces[pieces_per_group:]
        elif n >= 2 and not conv_done:
            norm_swish()
            conv_done = True
    assert conv_done


def _in_call(x2d, layer, mod, mod_row, norm_g, w_in, tabs, conv_dw, conv_dw_b, ln_g, ln_b,
             seq_len, tm):
    n_tok, d = x2d.shape
    seq_tiles = seq_len // tm
    halo_blocks = tm // HALO
    n_halo = n_tok // HALO
    if mod_row is None:
        row = lambda i: i // seq_tiles
    else:
        row = lambda i: mod_row
    tab_spec = pl.BlockSpec((tm, LANES), lambda i: (i % seq_tiles, 0))
    kern = functools.partial(_in_kernel, tm=tm, seq_tiles=seq_tiles, seq_len=seq_len)
    return pl.pallas_call(
        kern,
        grid=(n_tok // tm,),
        in_specs=[
            pl.BlockSpec((tm, d), lambda i: (i, 0)),
            pl.BlockSpec((HALO, d), lambda i: (jnp.maximum(i * halo_blocks - 1, 0), 0)),
            pl.BlockSpec((HALO, d), lambda i: (jnp.minimum((i + 1) * halo_blocks, n_halo - 1), 0)),
            pl.BlockSpec((None, None, 1, d), lambda i: (layer, row(i), 0, 1)),
            pl.BlockSpec((None, None, 1, d), lambda i: (layer, row(i), 0, 0)),
            _layer_spec(norm_g, layer, 1),
            _layer_spec(w_in, layer, 1),
            tab_spec, tab_spec, tab_spec, tab_spec,
            _layer_spec(conv_dw, layer, 1), _layer_spec(conv_dw_b, layer, 1),
            _layer_spec(ln_g, layer, 1), _layer_spec(ln_b, layer, 1),
        ],
        out_specs=pl.BlockSpec((tm, N_OUT_GROUPS * SEG), lambda i: (i, 0)),
        out_shape=jax.ShapeDtypeStruct((n_tok, N_OUT_GROUPS * SEG), BF16),
        scratch_shapes=[pltpu.VMEM((tm + 2 * HALO, d), BF16),
                        pltpu.VMEM((SLABS, tm + 2 * HALO, LANES), F32),
                        pltpu.VMEM((SLABS, tm + 2 * HALO, LANES), F32),
                        pltpu.VMEM((tm, SEG), F32)],
        compiler_params=pltpu.CompilerParams(
            dimension_semantics=("parallel",), vmem_limit_bytes=VMEM_LIMIT_BYTES),
        name="in_proj",
    )(x2d, x2d, x2d, mod, mod, norm_g, w_in, *tabs, conv_dw, conv_dw_b, ln_g, ln_b)


def _edge_class_rows(rows):
    n_dr = 2 * NA_KH - 1
    idx = np.full((N_EDGE_CLASSES, 2, KEY_ROWS), n_dr, np.int32)
    for c, r0 in enumerate((0, 2, 4, rows - 4, rows - 2)):
        ks = int(np.clip(r0 - NA_KH // 2, 0, rows - KEY_ROWS))
        for i in range(2):
            r = r0 + i
            rs = int(np.clip(r - NA_KH // 2, 0, rows - NA_KH))
            for jr in range(KEY_ROWS):
                kr = ks + jr
                if rs <= kr < rs + NA_KH:
                    idx[c, i, jr] = kr - r + NA_KH - 1
    return idx


def _att_kernel(q_ref, k_ref, v_ref, kc_ref, vc_ref, bd_ref, o_ref, tab, *, rows):
    rb = pl.program_id(2)
    pair_q = 2 * GRID_W
    n_pairs = ATT_ROWS_PER_STEP // 2
    nt = (((1,), (1,)), ((), ()))

    @pl.when(rb == 0)
    def _():
        idx = _edge_class_rows(rows)
        low = lax.broadcasted_iota(jnp.int32, (GRID_W, LANES), 1) < GRID_W
        for c in range(N_EDGE_CLASSES):
            for hh in range(2):
                for i in range(2):
                    r_lo = (2 * hh + i) * GRID_W
                    for m in range(KEY_ROWS // 2):
                        a, b = int(idx[c, i, 2 * m]), int(idx[c, i, 2 * m + 1])
                        tab[c, r_lo:r_lo + GRID_W, m * LANES:(m + 1) * LANES] = LOG2E * jnp.where(
                            low, bd_ref[hh, a], bd_ref[hh, b])

    kc = kc_ref[...]
    vc = vc_ref[...]
    first = lax.broadcasted_iota(jnp.int32, (pair_q, LANES), 1) < HEAD_DIM

    def scores(pp):
        r0 = rb * ATT_ROWS_PER_STEP + 2 * pp
        ks = jnp.clip(r0 - NA_KH // 2, 0, rows - KEY_ROWS)
        start = pl.multiple_of(ks * GRID_W, LANES)
        cls = jnp.where(r0 == 0, 0, jnp.where(r0 == 2, 1, jnp.where(
            r0 == rows - 4, 3, jnp.where(r0 == rows - 2, 4, 2))))
        q2 = q_ref[pp * pair_q:(pp + 1) * pair_q, :]
        zero = jnp.zeros_like(q2)
        qs = jnp.concatenate([jnp.where(first, q2, zero), jnp.where(first, zero, q2)], axis=0)
        kw = k_ref[pl.ds(start, KEY_ROWS * GRID_W), :]
        s_loc = lax.dot_general(qs, kw, nt, preferred_element_type=F32) + tab[cls]
        s_ctx = lax.dot_general(qs, kc, nt, preferred_element_type=F32)
        return s_loc, s_ctx, start

    def softmax(s_loc, s_ctx):
        m = jnp.maximum(s_loc.max(axis=-1, keepdims=True), s_ctx.max(axis=-1, keepdims=True))
        e_loc = jnp.exp2(s_loc - m)
        e_ctx = jnp.exp2(s_ctx - m)
        denom = e_loc.sum(axis=-1, keepdims=True) + e_ctx.sum(axis=-1, keepdims=True)
        return e_loc.astype(BF16), e_ctx.astype(BF16), denom

    def weighted_values(pp, e_loc, e_ctx, denom, start):
        vw = v_ref[pl.ds(start, KEY_ROWS * GRID_W), :]
        o = (jnp.dot(e_loc, vw, preferred_element_type=F32)
             + jnp.dot(e_ctx, vc, preferred_element_type=F32)) / denom
        out = jnp.where(first, o[:pair_q], o[pair_q:])
        o_ref[pp * pair_q:(pp + 1) * pair_q, :] = out.astype(BF16)

    pending = scores(0)
    for pp in range(n_pairs):
        s_loc, s_ctx, start = pending
        if pp + 1 < n_pairs:
            pending = scores(pp + 1)
        weighted_values(pp, *softmax(s_loc, s_ctx), start)


def _att_call(p_lat, p_ctx, layer, bias_blocks, batch, seq_len, ctx_len):
    n_tok = p_lat.shape[0]
    rows = seq_len // GRID_W
    heads2 = (2 * SEG) // LANES
    rb_per_seq = rows // ATT_ROWS_PER_STEP
    tq = ATT_ROWS_PER_STEP * GRID_W
    qb, kb, vb = (2 * SEG) // LANES, (4 * SEG) // LANES, (6 * SEG) // LANES
    n_dr = bias_blocks.shape[2]
    return pl.pallas_call(
        functools.partial(_att_kernel, rows=rows),
        grid=(batch, heads2, rb_per_seq),
        in_specs=[
            pl.BlockSpec((tq, LANES), lambda b, h, r: (b * rb_per_seq + r, qb + h)),
            pl.BlockSpec((seq_len, LANES), lambda b, h, r: (b, kb + h)),
            pl.BlockSpec((seq_len, LANES), lambda b, h, r: (b, vb + h)),
            pl.BlockSpec((ctx_len, LANES), lambda b, h, r: (b, kb + h)),
            pl.BlockSpec((ctx_len, LANES), lambda b, h, r: (b, vb + h)),
            pl.BlockSpec((None, 2, n_dr, GRID_W, LANES), lambda b, h, r: (layer, h, 0, 0, 0)),
        ],
        out_specs=pl.BlockSpec((tq, LANES), lambda b, h, r: (b * rb_per_seq + r, h)),
        out_shape=jax.ShapeDtypeStruct((n_tok, 2 * SEG), BF16),
        scratch_shapes=[pltpu.VMEM((N_EDGE_CLASSES, 4 * GRID_W, KEY_ROWS * GRID_W), F32)],
        compiler_params=pltpu.CompilerParams(
            dimension_semantics=("arbitrary", "arbitrary", "arbitrary"),
            vmem_limit_bytes=VMEM_LIMIT_BYTES),
        name="nbr_attention",
    )(p_lat, p_lat, p_lat, p_ctx, p_ctx, bias_blocks)


def _two_head_attention(q2, k2, v2):
    m_rows = q2.shape[0]
    first = lax.broadcasted_iota(jnp.int32, q2.shape, 1) < HEAD_DIM
    zero = jnp.zeros_like(q2)
    qs = jnp.concatenate([jnp.where(first, q2, zero), jnp.where(first, zero, q2)], axis=0)
    s = lax.dot_general(qs, k2, (((1,), (1,)), ((), ())), preferred_element_type=F32)
    e = jnp.exp2(s - s.max(axis=-1, keepdims=True))
    o = jnp.dot(e.astype(BF16), v2, preferred_element_type=F32) / e.sum(axis=-1, keepdims=True)
    return jnp.where(first, o[:m_rows], o[m_rows:])


def _ctx_att_kernel(q_ref, k_ref, v_ref, o_ref):
    o_ref[...] = _two_head_attention(q_ref[...], k_ref[...], v_ref[...]).astype(BF16)


def _ctx_att_call(p_ctx, batch, ctx_len):
    n_tok = p_ctx.shape[0]
    heads2 = (2 * SEG) // LANES
    qb, kb, vb = (2 * SEG) // LANES, (4 * SEG) // LANES, (6 * SEG) // LANES
    return pl.pallas_call(
        _ctx_att_kernel,
        grid=(batch, heads2),
        in_specs=[
            pl.BlockSpec((ctx_len, LANES), lambda b, h: (b, qb + h)),
            pl.BlockSpec((ctx_len, LANES), lambda b, h: (b, kb + h)),
            pl.BlockSpec((ctx_len, LANES), lambda b, h: (b, vb + h)),
        ],
        out_specs=pl.BlockSpec((ctx_len, LANES), lambda b, h: (b, h)),
        out_shape=jax.ShapeDtypeStruct((n_tok, 2 * SEG), BF16),
        compiler_params=pltpu.CompilerParams(
            dimension_semantics=("parallel", "parallel"), vmem_limit_bytes=VMEM_LIMIT_BYTES),
        name="ctx_attention",
    )(p_ctx, p_ctx, p_ctx)


def _bias_blocks(na_rpb):
    depth, heads, n_dr, n_dc = na_rpb.shape
    cq = np.arange(GRID_W)[:, None]
    ck = np.arange(GRID_W)[None, :]
    cs = np.clip(cq - NA_KW // 2, 0, GRID_W - NA_KW)
    col_ok = (ck >= cs) & (ck < cs + NA_KW)
    dc = ck - cq + NA_KW - 1
    onehot = ((np.arange(n_dc)[:, None, None] == dc[None]) & col_ok[None]).astype(np.float32)
    b1 = jnp.einsum("lhrd,dqk->lhrqk", na_rpb.astype(F32), jnp.asarray(onehot),
                    precision=lax.Precision.HIGHEST)
    b1 = jnp.where(col_ok, b1, MASK_VALUE)
    masked = jnp.full((depth, heads, 1, GRID_W, GRID_W), MASK_VALUE, F32)
    b1 = jnp.concatenate([b1, masked], axis=2)
    return jnp.concatenate([b1, b1], axis=-1)


def _mix_kernel(x_ref, gate_ref, pooled_ref, ag_ref, at_ref, bg_ref, sw_ref, cg_ref,
                wpool_ref, ps_ref, pw_ref, pwb_ref, wout_ref, fg_ref, o_ref, *, final):
    ya = jnp.dot(pooled_ref[...], wpool_ref[...], preferred_element_type=F32)
    ya = ya * ps_ref[...] * ag_ref[...].astype(F32)
    cm = jnp.dot(sw_ref[...], pw_ref[...], preferred_element_type=F32) + pwb_ref[...]
    ycc = cm * cg_ref[...].astype(F32)
    yb = at_ref[...] * bg_ref[...]
    cat = jnp.concatenate([ya.astype(BF16), yb, ycc.astype(BF16)], axis=1)
    mixed = jnp.dot(cat, wout_ref[...], preferred_element_type=F32)
    xn = x_ref[...] + gate_ref[...] * mixed
    if final:
        ms = jnp.mean(xn * xn, axis=-1, keepdims=True)
        xn = xn * lax.rsqrt(ms + NORM_EPS) * fg_ref[...]
    o_ref[...] = xn


def _mix_call(x2d, layer, mod, mod_row, p, att, wpool, pool_scale, conv_pw, conv_pw_b, w_out,
              final_g, seq_len, tm, final):
    n_tok, d = x2d.shape
    seq_tiles = seq_len // tm
    if mod_row is None:
        row = lambda i: i // seq_tiles
    else:
        row = lambda i: mod_row

    def group(colblk):
        return pl.BlockSpec((tm, SEG), lambda i: (i, colblk))

    return pl.pallas_call(
        functools.partial(_mix_kernel, final=final),
        grid=(n_tok // tm,),
        in_specs=[
            pl.BlockSpec((tm, d), lambda i: (i, 0)),
            pl.BlockSpec((None, None, 1, d), lambda i: (layer, row(i), 0, 2)),
            group(0),
            group(1),
            pl.BlockSpec((tm, 2 * SEG), lambda i: (i, 0)),
            pl.BlockSpec((tm, 2 * SEG), lambda i: (i, 4)),
            group(10),
            group(11),
            _layer_spec(wpool, layer, 1), _layer_spec(pool_scale, layer, 1),
            _layer_spec(conv_pw, layer, 1), _layer_spec(conv_pw_b, layer, 1),
            _layer_spec(w_out, layer, 1),
            pl.BlockSpec(final_g.shape, lambda i: (0, 0), pipeline_mode=pl.Buffered(1)),
        ],
        out_specs=pl.BlockSpec((tm, d), lambda i: (i, 0)),
        out_shape=jax.ShapeDtypeStruct((n_tok, d), F32),
        compiler_params=pltpu.CompilerParams(
            dimension_semantics=("parallel",), vmem_limit_bytes=VMEM_LIMIT_BYTES),
        name="mix_out",
    )(x2d, mod, p, p, att, p, p, p, wpool, pool_scale, conv_pw, conv_pw_b, w_out, final_g)


def _rope_tables(seq_len):
    half = HEAD_DIM // 2
    freqs = ROPE_THETA ** (-jnp.arange(0, half, 2, dtype=F32) / half)
    t = jnp.arange(seq_len)
    ang_r = (t // GRID_W).astype(F32)[:, None] * freqs[None, :]
    ang_c = (t % GRID_W).astype(F32)[:, None] * freqs[None, :]
    cos = jnp.concatenate([jnp.cos(ang_r)] * 2 + [jnp.cos(ang_c)] * 2, axis=-1)
    sin = jnp.concatenate([-jnp.sin(ang_r), jnp.sin(ang_r), -jnp.sin(ang_c), jnp.sin(ang_c)], axis=-1)
    cos = jnp.tile(cos, (1, LANES // HEAD_DIM))
    sin = jnp.tile(sin, (1, LANES // HEAD_DIM))
    return cos * Q_SCALE, sin * Q_SCALE, cos, sin


def kernel(x, c, ctx, c_ctx, w_mod, b_mod, norm_g, w_in, w_pool, pool_scale, na_rpb, conv_dw,
           conv_dw_b, conv_ln_g, conv_ln_b, conv_pw, conv_pw_b, w_out, final_norm_g):
    batch, seq_len, d = x.shape
    ctx_len = ctx.shape[1]
    depth = w_mod.shape[0]
    assert d == 4 * SEG and w_in.shape[2] == N_IN_GROUPS * SEG
    assert seq_len % (ATT_ROWS_PER_STEP * GRID_W) == 0 and seq_len // GRID_W >= 12
    assert batch + 1 <= 8 and w_pool.shape[-1] == LANES

    cstack = jnp.zeros((8, d), F32).at[:batch].set(c).at[batch].set(c_ctx)
    mod = _mod_call(cstack, w_mod, b_mod).reshape(depth, 8, 1, 3 * d)

    lat_tabs = _rope_tables(seq_len)
    ones = jnp.ones((ctx_len, LANES), F32)
    ctx_tabs = (ones * Q_SCALE, jnp.zeros_like(ones), ones, jnp.zeros_like(ones))
    bias_blocks = _bias_blocks(na_rpb)

    groups = len(POOL_WINDOWS)
    pg = w_pool.shape[-1]
    eye = jnp.eye(groups, dtype=F32)
    wpool_bd = (eye[None, :, None, :, None] * w_pool[:, :, :, None, :]).reshape(
        depth, groups * pg, groups * pg).astype(BF16)
    w_in_b = w_in.astype(BF16)
    w_out_b = w_out.astype(BF16)
    conv_pw_b16 = conv_pw.astype(BF16)
    row_vec = lambda a: a.reshape(depth, 1, a.shape[-1])
    in_w = (row_vec(norm_g), w_in_b)
    conv_w = (conv_dw, row_vec(conv_dw_b), row_vec(conv_ln_g), row_vec(conv_ln_b))
    mix_w = (wpool_bd, row_vec(pool_scale), conv_pw_b16, row_vec(conv_pw_b), w_out_b,
             final_norm_g.reshape(1, d))

    tm_in = min(256, seq_len)
    tm_mix = min(512, seq_len)
    xl = x.reshape(batch * seq_len, d)
    xc = ctx.reshape(batch * ctx_len, d)
    for i in range(depth):
        last = i == depth - 1
        p_ctx = _in_call(xc, i, mod, batch, *in_w, ctx_tabs, *conv_w, ctx_len, min(tm_in, ctx_len))
        p_lat = _in_call(xl, i, mod, None, *in_w, lat_tabs, *conv_w, seq_len, tm_in)
        att = _att_call(p_lat, p_ctx, i, bias_blocks, batch, seq_len, ctx_len)
        xl = _mix_call(xl, i, mod, None, p_lat, att, *mix_w, seq_len, tm_mix, last)
        if not last:
            att_c = _ctx_att_call(p_ctx, batch, ctx_len)
            xc = _mix_call(xc, i, mod, batch, p_ctx, att_c, *mix_w, ctx_len, ctx_len, False)
    return xl.reshape(batch, seq_len, d)
```

```python
import functools

import numpy as np
import jax
import jax.numpy as jnp
from jax import lax
from jax.experimental import pallas as pl
from jax.experimental.pallas import tpu as pltpu

F32 = jnp.float32
BF16 = jnp.bfloat16

GRID_W = 64
POOL_WINDOWS = (2, 4, 8, 16)
HEAD_DIM = 64
NA_KH = 8
NA_KW = 16
ROPE_THETA = 10000.0
CONV_K = 31
NORM_EPS = 1e-6
LN_EPS = 1e-5

LANES = 128
VMEM_LIMIT_BYTES = 56 * 1024 * 1024

SEG = 512
SLABS = SEG // LANES
SUB = 256
HALO = 16
MASK_VALUE = -1e30
KEY_ROWS = 10
N_EDGE_CLASSES = 5
ATT_ROWS_PER_STEP = 32
LOG2E = 1.4426950408889634
Q_SCALE = HEAD_DIM ** -0.5 * LOG2E
CONV_CHUNK = 32


def _sigmoid(v):
    return 1.0 / (1.0 + jnp.exp(-v))


def _silu(v):
    return v * _sigmoid(v)


def _layer_spec(a, layer, grid_rank):
    zeros = (0,) * (a.ndim - 1)
    if grid_rank == 1:
        index_map = lambda i: (layer,) + zeros
    else:
        index_map = lambda i, j: (layer,) + zeros
    return pl.BlockSpec((None,) + a.shape[1:], index_map, pipeline_mode=pl.Buffered(1))


def _mod_kernel(c_ref, w_ref, b_ref, o_ref):
    a = _silu(c_ref[...]).astype(BF16)
    o_ref[0] = jnp.dot(a, w_ref[0].astype(BF16), preferred_element_type=F32) + b_ref[0]


def _mod_call(cstack, w_mod, b_mod):
    depth, d, n = w_mod.shape
    tn = 768
    return pl.pallas_call(
        _mod_kernel,
        grid=(depth, n // tn),
        in_specs=[
            pl.BlockSpec((8, d), lambda l, j: (0, 0)),
            pl.BlockSpec((1, d, tn), lambda l, j: (l, 0, j)),
            pl.BlockSpec((1, 1, tn), lambda l, j: (l, 0, j)),
        ],
        out_specs=pl.BlockSpec((1, 8, tn), lambda l, j: (l, 0, j)),
        out_shape=jax.ShapeDtypeStruct((depth, 8, n), F32),
        compiler_params=pltpu.CompilerParams(
            dimension_semantics=("parallel", "parallel"), vmem_limit_bytes=VMEM_LIMIT_BYTES),
        name="adaln_mod",
    )(cstack, w_mod, b_mod.reshape(depth, 1, n))


_POOL, _SILU, _ROPE_Q, _ROPE_K, _PLAIN, _GLU_A, _GLU_B = range(7)
IN_GROUP_KINDS = (_POOL, _SILU, _ROPE_Q, _ROPE_Q, _ROPE_K, _ROPE_K, _PLAIN, _PLAIN,
                  _SILU, _SILU, _GLU_A, _GLU_B, _SILU)
N_IN_GROUPS = len(IN_GROUP_KINDS)
N_OUT_GROUPS = N_IN_GROUPS - 1
IN_GROUP_ORDER = (10, 11) + tuple(range(10)) + (12,)
NEEDS_HALO = (_POOL, _GLU_A, _GLU_B)


def _rope(r, cos_ref, sin_ref):
    n = r.shape[1]
    reps = n // LANES
    lane = lax.broadcasted_iota(jnp.int32, r.shape, 1)
    up = pltpu.roll(r, n - 16, axis=1)
    dn = pltpu.roll(r, 16, axis=1)
    partner = jnp.where((lane % 32) < 16, up, dn)
    cos = jnp.tile(cos_ref[...], (1, reps))
    sin = jnp.tile(sin_ref[...], (1, reps))
    return r * cos + partner * sin


def _in_kernel(x_ref, xp_ref, xn_ref, scale_ref, shift_ref, g_ref, w_ref,
               cq_ref, sq_ref, ck_ref, sk_ref, dw_ref, dwb_ref, lng_ref, lnb_ref,
               o_ref, h_scr, exa, exy, ycv, *, tm, seq_tiles, seq_len):
    s = pl.program_id(0) % seq_tiles
    first = s == 0
    last = s == seq_tiles - 1
    ext_rows = tm + 2 * HALO

    gain = g_ref[...] * (1.0 + scale_ref[...])
    shift = shift_ref[...]

    def normed(xv):
        ms = jnp.mean(xv * xv, axis=-1, keepdims=True)
        return (xv * lax.rsqrt(ms + NORM_EPS) * gain + shift).astype(BF16)

    h_scr[0:HALO, :] = normed(xp_ref[...])
    h_scr[HALO:HALO + tm, :] = normed(x_ref[...])
    h_scr[HALO + tm:, :] = normed(xn_ref[...])

    rid = lax.broadcasted_iota(jnp.int32, (ext_rows, LANES), 0)
    inside = (rid >= jnp.where(first, HALO, 0)) & (rid < jnp.where(last, HALO + tm, ext_rows))

    def project(j, half):
        lo = j * SEG + half * SUB
        if IN_GROUP_KINDS[j] in NEEDS_HALO:
            return jnp.dot(h_scr[...], w_ref[:, lo:lo + SUB], preferred_element_type=F32)
        return jnp.dot(h_scr[HALO:HALO + tm, :], w_ref[:, lo:lo + SUB],
                       preferred_element_type=F32)

    def slabs_of(half):
        return range(half * SUB // LANES, (half + 1) * SUB // LANES)

    def pool_epilogue(r, half):
        t = s * tm + lax.broadcasted_iota(jnp.int32, (tm, LANES), 0)
        for n_g, g in enumerate(slabs_of(half)):
            exa[g] = jnp.where(inside, r[:, n_g * LANES:(n_g + 1) * LANES], 0.0)
            w = POOL_WINDOWS[g]
            before, after = w // 2, w - w // 2 - 1
            total = exa[g, HALO - before:HALO - before + tm, :]
            for dlt in range(-before + 1, after + 1):
                total = total + exa[g, HALO + dlt:HALO + dlt + tm, :]
            lo = jnp.clip(t - before, 0, seq_len - 1)
            hi = jnp.clip(t + after, 0, seq_len - 1)
            cnt = (hi - lo + 1).astype(F32)
            o_ref[:, g * LANES:(g + 1) * LANES] = (
                total / cnt - exa[g, HALO:HALO + tm, :]).astype(BF16)

    def glu_epilogue(r, half):
        for n_g, g in enumerate(slabs_of(half)):
            gate = _sigmoid(r[:, n_g * LANES:(n_g + 1) * LANES])
            exy[g] = jnp.where(inside, exy[g] * gate, 0.0)

    def conv_piece(c, g, anchor):
        base = c * CONV_CHUNK + HALO - CONV_K // 2
        cols = slice(g * LANES, (g + 1) * LANES)
        acc = dwb_ref[:, cols]
        for k in range(CONV_K):
            tap = jnp.tile(dw_ref[k:k + 1, cols] + anchor, (CONV_CHUNK // 8, 1))
            acc = acc + exy[g, base + k:base + k + CONV_CHUNK, :] * tap
        ycv[c * CONV_CHUNK:(c + 1) * CONV_CHUNK, cols] = acc

    def exact_zero_from(r):
        bits = pltpu.bitcast(r[0:8, 0:LANES], jnp.uint32)
        half = jnp.uint32(16)
        gone = lax.shift_right_logical(lax.shift_right_logical(bits, half), half)
        return pltpu.bitcast(gone, F32)

    def norm_swish():
        yc = ycv[...]
        mu = jnp.mean(yc, axis=-1, keepdims=True)
        dev = yc - mu
        var = jnp.mean(dev * dev, axis=-1, keepdims=True)
        yn = dev * lax.rsqrt(var + LN_EPS) * lng_ref[...] + lnb_ref[...]
        o_ref[:, 10 * SEG:11 * SEG] = _silu(yn).astype(BF16)

    units = [(j, half) for j in IN_GROUP_ORDER for half in range(SEG // SUB)]
    n_units = len(units)
    pieces = []
    pieces_per_unit = -(-(tm // CONV_CHUNK) * SLABS // (n_units - 8))
    conv_done = False

    pending = project(*units[0])
    for n, (j, half) in enumerate(units):
        r = pending
        if n + 1 < n_units:
            pending = project(*units[n + 1])
        kind = IN_GROUP_KINDS[j]
        if kind == _GLU_A:
            for n_g, g in enumerate(slabs_of(half)):
                exy[g] = r[:, n_g * LANES:(n_g + 1) * LANES]
        elif kind == _GLU_B:
            glu_epilogue(r, half)
            pieces += [(c, g) for c in range(tm // CONV_CHUNK) for g in slabs_of(half)]
        elif kind == _POOL:
            pool_epilogue(r, half)
        else:
            if kind == _PLAIN:
                out = r
            elif kind == _SILU:
                out = _silu(r)
            elif kind == _ROPE_Q:
                out = _rope(r, cq_ref, sq_ref)
            else:
                out = _rope(r, ck_ref, sk_ref)
            col = j if j < 10 else j - 1
            o_ref[:, col * SEG + half * SUB:col * SEG + (half + 1) * SUB] = out.astype(BF16)
        if kind not in (_GLU_A, _GLU_B):
            if pieces:
                anchor = exact_zero_from(r)
                for c, g in pieces[:pieces_per_unit]:
                    conv_piece(c, g, anchor)
                pieces = pieces[pieces_per_unit:]
            elif not conv_done:
                norm_swish()
                conv_done = True
    assert conv_done and not pieces


def _in_call(x2d, layer, mod, mod_row, norm_g, w_in, tabs, conv_dw, conv_dw_b, ln_g, ln_b,
             seq_len, tm):
    n_tok, d = x2d.shape
    seq_tiles = seq_len // tm
    halo_blocks = tm // HALO
    n_halo = n_tok // HALO
    if mod_row is None:
        row = lambda i: i // seq_tiles
    else:
        row = lambda i: mod_row
    tab_spec = pl.BlockSpec((tm, LANES), lambda i: (i % seq_tiles, 0))
    kern = functools.partial(_in_kernel, tm=tm, seq_tiles=seq_tiles, seq_len=seq_len)
    return pl.pallas_call(
        kern,
        grid=(n_tok // tm,),
        in_specs=[
            pl.BlockSpec((tm, d), lambda i: (i, 0)),
            pl.BlockSpec((HALO, d), lambda i: (jnp.maximum(i * halo_blocks - 1, 0), 0)),
            pl.BlockSpec((HALO, d), lambda i: (jnp.minimum((i + 1) * halo_blocks, n_halo - 1), 0)),
            pl.BlockSpec((None, None, 1, d), lambda i: (layer, row(i), 0, 1)),
            pl.BlockSpec((None, None, 1, d), lambda i: (layer, row(i), 0, 0)),
            _layer_spec(norm_g, layer, 1),
            _layer_spec(w_in, layer, 1),
            tab_spec, tab_spec, tab_spec, tab_spec,
            _layer_spec(conv_dw, layer, 1), _layer_spec(conv_dw_b, layer, 1),
            _layer_spec(ln_g, layer, 1), _layer_spec(ln_b, layer, 1),
        ],
        out_specs=pl.BlockSpec((tm, N_OUT_GROUPS * SEG), lambda i: (i, 0)),
        out_shape=jax.ShapeDtypeStruct((n_tok, N_OUT_GROUPS * SEG), BF16),
        scratch_shapes=[pltpu.VMEM((tm + 2 * HALO, d), BF16),
                        pltpu.VMEM((SLABS, tm + 2 * HALO, LANES), F32),
                        pltpu.VMEM((SLABS, tm + 2 * HALO, LANES), F32),
                        pltpu.VMEM((tm, SEG), F32)],
        compiler_params=pltpu.CompilerParams(
            dimension_semantics=("parallel",), vmem_limit_bytes=VMEM_LIMIT_BYTES),
        name="in_proj",
    )(x2d, x2d, x2d, mod, mod, norm_g, w_in, *tabs, conv_dw, conv_dw_b, ln_g, ln_b)


def _edge_class_rows(rows):
    n_dr = 2 * NA_KH - 1
    idx = np.full((N_EDGE_CLASSES, 2, KEY_ROWS), n_dr, np.int32)
    for c, r0 in enumerate((0, 2, 4, rows - 4, rows - 2)):
        ks = int(np.clip(r0 - NA_KH // 2, 0, rows - KEY_ROWS))
        for i in range(2):
            r = r0 + i
            rs = int(np.clip(r - NA_KH // 2, 0, rows - NA_KH))
            for jr in range(KEY_ROWS):
                kr = ks + jr
                if rs <= kr < rs + NA_KH:
                    idx[c, i, jr] = kr - r + NA_KH - 1
    return idx


def _att_kernel(q_ref, k_ref, v_ref, kc_ref, vc_ref, bd_ref, o_ref, tab, *, rows):
    rb = pl.program_id(2)
    pair_q = 2 * GRID_W
    n_pairs = ATT_ROWS_PER_STEP // 2
    nt = (((1,), (1,)), ((), ()))

    @pl.when(rb == 0)
    def _():
        idx = _edge_class_rows(rows)
        low = lax.broadcasted_iota(jnp.int32, (GRID_W, LANES), 1) < GRID_W
        for c in range(N_EDGE_CLASSES):
            for hh in range(2):
                for i in range(2):
                    r_lo = (2 * hh + i) * GRID_W
                    for m in range(KEY_ROWS // 2):
                        a, b = int(idx[c, i, 2 * m]), int(idx[c, i, 2 * m + 1])
                        tab[c, r_lo:r_lo + GRID_W, m * LANES:(m + 1) * LANES] = LOG2E * jnp.where(
                            low, bd_ref[hh, a], bd_ref[hh, b])

    kc = kc_ref[...]
    vc = vc_ref[...]
    first = lax.broadcasted_iota(jnp.int32, (pair_q, LANES), 1) < HEAD_DIM

    def scores(pp):
        r0 = rb * ATT_ROWS_PER_STEP + 2 * pp
        ks = jnp.clip(r0 - NA_KH // 2, 0, rows - KEY_ROWS)
        start = pl.multiple_of(ks * GRID_W, LANES)
        cls = jnp.where(r0 == 0, 0, jnp.where(r0 == 2, 1, jnp.where(
            r0 == rows - 4, 3, jnp.where(r0 == rows - 2, 4, 2))))
        q2 = q_ref[pp * pair_q:(pp + 1) * pair_q, :]
        zero = jnp.zeros_like(q2)
        qs = jnp.concatenate([jnp.where(first, q2, zero), jnp.where(first, zero, q2)], axis=0)
        kw = k_ref[pl.ds(start, KEY_ROWS * GRID_W), :]
        s_loc = lax.dot_general(qs, kw, nt, preferred_element_type=F32) + tab[cls]
        s_ctx = lax.dot_general(qs, kc, nt, preferred_element_type=F32)
        return s_loc, s_ctx, start

    def softmax(s_loc, s_ctx):
        m = jnp.maximum(s_loc.max(axis=-1, keepdims=True), s_ctx.max(axis=-1, keepdims=True))
        e_loc = jnp.exp2(s_loc - m)
        e_ctx = jnp.exp2(s_ctx - m)
        denom = e_loc.sum(axis=-1, keepdims=True) + e_ctx.sum(axis=-1, keepdims=True)
        return e_loc.astype(BF16), e_ctx.astype(BF16), denom

    def weighted_values(pp, e_loc, e_ctx, denom, start):
        vw = v_ref[pl.ds(start, KEY_ROWS * GRID_W), :]
        o = (jnp.dot(e_loc, vw, preferred_element_type=F32)
             + jnp.dot(e_ctx, vc, preferred_element_type=F32)) / denom
        out = jnp.where(first, o[:pair_q], o[pair_q:])
        o_ref[pp * pair_q:(pp + 1) * pair_q, :] = out.astype(BF16)

    pending = scores(0)
    for pp in range(n_pairs):
        s_loc, s_ctx, start = pending
        if pp + 1 < n_pairs:
            pending = scores(pp + 1)
        weighted_values(pp, *softmax(s_loc, s_ctx), start)


def _att_call(p_lat, p_ctx, layer, bias_blocks, batch, seq_len, ctx_len):
    n_tok = p_lat.shape[0]
    rows = seq_len // GRID_W
    heads2 = (2 * SEG) // LANES
    rb_per_seq = rows // ATT_ROWS_PER_STEP
    tq = ATT_ROWS_PER_STEP * GRID_W
    qb, kb, vb = (2 * SEG) // LANES, (4 * SEG) // LANES, (6 * SEG) // LANES
    n_dr = bias_blocks.shape[2]
    return pl.pallas_call(
        functools.partial(_att_kernel, rows=rows),
        grid=(batch, heads2, rb_per_seq),
        in_specs=[
            pl.BlockSpec((tq, LANES), lambda b, h, r: (b * rb_per_seq + r, qb + h)),
            pl.BlockSpec((seq_len, LANES), lambda b, h, r: (b, kb + h)),
            pl.BlockSpec((seq_len, LANES), lambda b, h, r: (b, vb + h)),
            pl.BlockSpec((ctx_len, LANES), lambda b, h, r: (b, kb + h)),
            pl.BlockSpec((ctx_len, LANES), lambda b, h, r: (b, vb + h)),
            pl.BlockSpec((None, 2, n_dr, GRID_W, LANES), lambda b, h, r: (layer, h, 0, 0, 0)),
        ],
        out_specs=pl.BlockSpec((tq, LANES), lambda b, h, r: (b * rb_per_seq + r, h)),
        out_shape=jax.ShapeDtypeStruct((n_tok, 2 * SEG), BF16),
        scratch_shapes=[pltpu.VMEM((N_EDGE_CLASSES, 4 * GRID_W, KEY_ROWS * GRID_W), F32)],
        compiler_params=pltpu.CompilerParams(
            dimension_semantics=("arbitrary", "arbitrary", "arbitrary"),
            vmem_limit_bytes=VMEM_LIMIT_BYTES),
        name="nbr_attention",
    )(p_lat, p_lat, p_lat, p_ctx, p_ctx, bias_blocks)


def _two_head_attention(q2, k2, v2):
    m_rows = q2.shape[0]
    first = lax.broadcasted_iota(jnp.int32, q2.shape, 1) < HEAD_DIM
    zero = jnp.zeros_like(q2)
    qs = jnp.concatenate([jnp.where(first, q2, zero), jnp.where(first, zero, q2)], axis=0)
    s = lax.dot_general(qs, k2, (((1,), (1,)), ((), ())), preferred_element_type=F32)
    e = jnp.exp2(s - s.max(axis=-1, keepdims=True))
    o = jnp.dot(e.astype(BF16), v2, preferred_element_type=F32) / e.sum(axis=-1, keepdims=True)
    return jnp.where(first, o[:m_rows], o[m_rows:])


def _ctx_att_kernel(q_ref, k_ref, v_ref, o_ref):
    o_ref[...] = _two_head_attention(q_ref[...], k_ref[...], v_ref[...]).astype(BF16)


def _ctx_att_call(p_ctx, batch, ctx_len):
    n_tok = p_ctx.shape[0]
    heads2 = (2 * SEG) // LANES
    qb, kb, vb = (2 * SEG) // LANES, (4 * SEG) // LANES, (6 * SEG) // LANES
    return pl.pallas_call(
        _ctx_att_kernel,
        grid=(batch, heads2),
        in_specs=[
            pl.BlockSpec((ctx_len, LANES), lambda b, h: (b, qb + h)),
            pl.BlockSpec((ctx_len, LANES), lambda b, h: (b, kb + h)),
            pl.BlockSpec((ctx_len, LANES), lambda b, h: (b, vb + h)),
        ],
        out_specs=pl.BlockSpec((ctx_len, LANES), lambda b, h: (b, h)),
        out_shape=jax.ShapeDtypeStruct((n_tok, 2 * SEG), BF16),
        compiler_params=pltpu.CompilerParams(
            dimension_semantics=("parallel", "parallel"), vmem_limit_bytes=VMEM_LIMIT_BYTES),
        name="ctx_attention",
    )(p_ctx, p_ctx, p_ctx)


def _bias_blocks(na_rpb):
    depth, heads, n_dr, n_dc = na_rpb.shape
    cq = np.arange(GRID_W)[:, None]
    ck = np.arange(GRID_W)[None, :]
    cs = np.clip(cq - NA_KW // 2, 0, GRID_W - NA_KW)
    col_ok = (ck >= cs) & (ck < cs + NA_KW)
    dc = ck - cq + NA_KW - 1
    onehot = ((np.arange(n_dc)[:, None, None] == dc[None]) & col_ok[None]).astype(np.float32)
    b1 = jnp.einsum("lhrd,dqk->lhrqk", na_rpb.astype(F32), jnp.asarray(onehot),
                    precision=lax.Precision.HIGHEST)
    b1 = jnp.where(col_ok, b1, MASK_VALUE)
    masked = jnp.full((depth, heads, 1, GRID_W, GRID_W), MASK_VALUE, F32)
    b1 = jnp.concatenate([b1, masked], axis=2)
    return jnp.concatenate([b1, b1], axis=-1)


def _mix_kernel(x_ref, gate_ref, pooled_ref, ag_ref, at_ref, bg_ref, sw_ref, cg_ref,
                wpool_ref, ps_ref, pw_ref, pwb_ref, wout_ref, fg_ref, o_ref, *, final):
    ya = jnp.dot(pooled_ref[...], wpool_ref[...], preferred_element_type=F32)
    ya = ya * ps_ref[...] * ag_ref[...].astype(F32)
    cm = jnp.dot(sw_ref[...], pw_ref[...], preferred_element_type=F32) + pwb_ref[...]
    ycc = cm * cg_ref[...].astype(F32)
    yb = at_ref[...] * bg_ref[...]
    cat = jnp.concatenate([ya.astype(BF16), yb, ycc.astype(BF16)], axis=1)
    mixed = jnp.dot(cat, wout_ref[...], preferred_element_type=F32)
    xn = x_ref[...] + gate_ref[...] * mixed
    if final:
        ms = jnp.mean(xn * xn, axis=-1, keepdims=True)
        xn = xn * lax.rsqrt(ms + NORM_EPS) * fg_ref[...]
    o_ref[...] = xn


def _mix_call(x2d, layer, mod, mod_row, p, att, wpool, pool_scale, conv_pw, conv_pw_b, w_out,
              final_g, seq_len, tm, final):
    n_tok, d = x2d.shape
    seq_tiles = seq_len // tm
    if mod_row is None:
        row = lambda i: i // seq_tiles
    else:
        row = lambda i: mod_row

    def group(colblk):
        return pl.BlockSpec((tm, SEG), lambda i: (i, colblk))

    return pl.pallas_call(
        functools.partial(_mix_kernel, final=final),
        grid=(n_tok // tm,),
        in_specs=[
            pl.BlockSpec((tm, d), lambda i: (i, 0)),
            pl.BlockSpec((None, None, 1, d), lambda i: (layer, row(i), 0, 2)),
            group(0),
            group(1),
            pl.BlockSpec((tm, 2 * SEG), lambda i: (i, 0)),
            pl.BlockSpec((tm, 2 * SEG), lambda i: (i, 4)),
            group(10),
            group(11),
            _layer_spec(wpool, layer, 1), _layer_spec(pool_scale, layer, 1),
            _layer_spec(conv_pw, layer, 1), _layer_spec(conv_pw_b, layer, 1),
            _layer_spec(w_out, layer, 1),
            pl.BlockSpec(final_g.shape, lambda i: (0, 0), pipeline_mode=pl.Buffered(1)),
        ],
        out_specs=pl.BlockSpec((tm, d), lambda i: (i, 0)),
        out_shape=jax.ShapeDtypeStruct((n_tok, d), F32),
        compiler_params=pltpu.CompilerParams(
            dimension_semantics=("parallel",), vmem_limit_bytes=VMEM_LIMIT_BYTES),
        name="mix_out",
    )(x2d, mod, p, p, att, p, p, p, wpool, pool_scale, conv_pw, conv_pw_b, w_out, final_g)


def _rope_tables(seq_len):
    half = HEAD_DIM // 2
    freqs = ROPE_THETA ** (-jnp.arange(0, half, 2, dtype=F32) / half)
    t = jnp.arange(seq_len)
    ang_r = (t // GRID_W).astype(F32)[:, None] * freqs[None, :]
    ang_c = (t % GRID_W).astype(F32)[:, None] * freqs[None, :]
    cos = jnp.concatenate([jnp.cos(ang_r)] * 2 + [jnp.cos(ang_c)] * 2, axis=-1)
    sin = jnp.concatenate([-jnp.sin(ang_r), jnp.sin(ang_r), -jnp.sin(ang_c), jnp.sin(ang_c)], axis=-1)
    cos = jnp.tile(cos, (1, LANES // HEAD_DIM))
    sin = jnp.tile(sin, (1, LANES // HEAD_DIM))
    return cos * Q_SCALE, sin * Q_SCALE, cos, sin


def kernel(x, c, ctx, c_ctx, w_mod, b_mod, norm_g, w_in, w_pool, pool_scale, na_rpb, conv_dw,
           conv_dw_b, conv_ln_g, conv_ln_b, conv_pw, conv_pw_b, w_out, final_norm_g):
    batch, seq_len, d = x.shape
    ctx_len = ctx.shape[1]
    depth = w_mod.shape[0]
    assert d == 4 * SEG and w_in.shape[2] == N_IN_GROUPS * SEG
    assert seq_len % (ATT_ROWS_PER_STEP * GRID_W) == 0 and seq_len // GRID_W >= 12
    assert batch + 1 <= 8 and w_pool.shape[-1] == LANES

    cstack = jnp.zeros((8, d), F32).at[:batch].set(c).at[batch].set(c_ctx)
    mod = _mod_call(cstack, w_mod, b_mod).reshape(depth, 8, 1, 3 * d)

    lat_tabs = _rope_tables(seq_len)
    ones = jnp.ones((ctx_len, LANES), F32)
    ctx_tabs = (ones * Q_SCALE, jnp.zeros_like(ones), ones, jnp.zeros_like(ones))
    bias_blocks = _bias_blocks(na_rpb)

    groups = len(POOL_WINDOWS)
    pg = w_pool.shape[-1]
    eye = jnp.eye(groups, dtype=F32)
    wpool_bd = (eye[None, :, None, :, None] * w_pool[:, :, :, None, :]).reshape(
        depth, groups * pg, groups * pg).astype(BF16)
    w_in_b = w_in.astype(BF16)
    w_out_b = w_out.astype(BF16)
    conv_pw_b16 = conv_pw.astype(BF16)
    row_vec = lambda a: a.reshape(depth, 1, a.shape[-1])
    in_w = (row_vec(norm_g), w_in_b)
    conv_w = (conv_dw, row_vec(conv_dw_b), row_vec(conv_ln_g), row_vec(conv_ln_b))
    mix_w = (wpool_bd, row_vec(pool_scale), conv_pw_b16, row_vec(conv_pw_b), w_out_b,
             final_norm_g.reshape(1, d))

    tm_in = min(256, seq_len)
    tm_mix = min(512, seq_len)
    xl = x.reshape(batch * seq_len, d)
    xc = ctx.reshape(batch * ctx_len, d)
    for i in range(depth):
        last = i == depth - 1
        p_ctx = _in_call(xc, i, mod, batch, *in_w, ctx_tabs, *conv_w, ctx_len, min(tm_in, ctx_len))
        p_lat = _in_call(xl, i, mod, None, *in_w, lat_tabs, *conv_w, seq_len, tm_in)
        att = _att_call(p_lat, p_ctx, i, bias_blocks, batch, seq_len, ctx_len)
        xl = _mix_call(xl, i, mod, None, p_lat, att, *mix_w, seq_len, tm_mix, last)
        if not last:
            att_c = _ctx_att_call(p_ctx, batch, ctx_len)
            xc = _mix_call(xc, i, mod, batch, p_ctx, att_c, *mix_w, ctx_len, ctx_len, False)
    return xl.reshape(batch, seq_len, d)
```

```python
import functools

import numpy as np
import jax
import jax.numpy as jnp
from jax import lax
from jax.experimental import pallas as pl
from jax.experimental.pallas import tpu as pltpu

F32 = jnp.float32
BF16 = jnp.bfloat16

GRID_W = 64
POOL_WINDOWS = (2, 4, 8, 16)
HEAD_DIM = 64
NA_KH = 8
NA_KW = 16
ROPE_THETA = 10000.0
CONV_K = 31
NORM_EPS = 1e-6
LN_EPS = 1e-5

LANES = 128
VMEM_LIMIT_BYTES = 56 * 1024 * 1024

SEG = 512
SLABS = SEG // LANES
SUB = 256
HALO = 16
MASK_VALUE = -1e30
KEY_ROWS = 10
N_EDGE_CLASSES = 5
ATT_ROWS_PER_STEP = 32
LOG2E = 1.4426950408889634
Q_SCALE = HEAD_DIM ** -0.5 * LOG2E
CONV_CHUNK = 32


def _sigmoid(v):
    return 1.0 / (1.0 + jnp.exp(-v))


def _silu(v):
    return v * _sigmoid(v)


def _layer_spec(a, layer, grid_rank):
    zeros = (0,) * (a.ndim - 1)
    if grid_rank == 1:
        index_map = lambda i: (layer,) + zeros
    else:
        index_map = lambda i, j: (layer,) + zeros
    return pl.BlockSpec((None,) + a.shape[1:], index_map, pipeline_mode=pl.Buffered(1))


def _mod_kernel(c_ref, w_ref, b_ref, o_ref):
    a = _silu(c_ref[...]).astype(BF16)
    o_ref[0] = jnp.dot(a, w_ref[0].astype(BF16), preferred_element_type=F32) + b_ref[0]


def _mod_call(cstack, w_mod, b_mod):
    depth, d, n = w_mod.shape
    tn = 768
    return pl.pallas_call(
        _mod_kernel,
        grid=(depth, n // tn),
        in_specs=[
            pl.BlockSpec((8, d), lambda l, j: (0, 0)),
            pl.BlockSpec((1, d, tn), lambda l, j: (l, 0, j)),
            pl.BlockSpec((1, 1, tn), lambda l, j: (l, 0, j)),
        ],
        out_specs=pl.BlockSpec((1, 8, tn), lambda l, j: (l, 0, j)),
        out_shape=jax.ShapeDtypeStruct((depth, 8, n), F32),
        compiler_params=pltpu.CompilerParams(
            dimension_semantics=("parallel", "parallel"), vmem_limit_bytes=VMEM_LIMIT_BYTES),
        name="adaln_mod",
    )(cstack, w_mod, b_mod.reshape(depth, 1, n))


_POOL, _SILU, _ROPE_Q, _ROPE_K, _PLAIN, _GLU_A, _GLU_B = range(7)
IN_GROUP_KINDS = (_POOL, _SILU, _ROPE_Q, _ROPE_Q, _ROPE_K, _ROPE_K, _PLAIN, _PLAIN,
                  _SILU, _SILU, _GLU_A, _GLU_B, _SILU)
N_IN_GROUPS = len(IN_GROUP_KINDS)
OUT_SLOT = {0: ("p", 0), 1: ("p", 1), 2: ("qkv", 0), 3: ("qkv", 4), 4: ("qkv", 8), 5: ("qkv", 12),
            6: ("qkv", 16), 7: ("qkv", 20), 8: ("p", 2), 9: ("p", 3), 11: ("p", 4), 12: ("p", 5)}
N_P_GROUPS = 6
N_QKV_BLOCKS = 24
Q_BLOCK, K_BLOCK, V_BLOCK = 0, 8, 16
IN_GROUP_ORDER = (10, 11) + tuple(range(10)) + (12,)
NEEDS_HALO = (_POOL, _GLU_A, _GLU_B)


def _rope(r, cos_ref, sin_ref):
    n = r.shape[1]
    reps = n // LANES
    lane = lax.broadcasted_iota(jnp.int32, r.shape, 1)
    up = pltpu.roll(r, n - 16, axis=1)
    dn = pltpu.roll(r, 16, axis=1)
    partner = jnp.where((lane % 32) < 16, up, dn)
    cos = jnp.tile(cos_ref[...], (1, reps))
    sin = jnp.tile(sin_ref[...], (1, reps))
    return r * cos + partner * sin


def _in_kernel(x_ref, xp_ref, xn_ref, scale_ref, shift_ref, g_ref, w_ref,
               cq_ref, sq_ref, ck_ref, sk_ref, dw_ref, dwb_ref, lng_ref, lnb_ref,
               o_ref, qkv_ref, h_scr, exa, exy, ycv, *, tm, seq_tiles, seq_len):
    s = pl.program_id(0) % seq_tiles
    first = s == 0
    last = s == seq_tiles - 1
    ext_rows = tm + 2 * HALO

    gain = g_ref[...] * (1.0 + scale_ref[...])
    shift = shift_ref[...]

    def normed(xv):
        ms = jnp.mean(xv * xv, axis=-1, keepdims=True)
        return (xv * lax.rsqrt(ms + NORM_EPS) * gain + shift).astype(BF16)

    h_scr[0:HALO, :] = normed(xp_ref[...])
    h_scr[HALO:HALO + tm, :] = normed(x_ref[...])
    h_scr[HALO + tm:, :] = normed(xn_ref[...])

    rid = lax.broadcasted_iota(jnp.int32, (ext_rows, LANES), 0)
    inside = (rid >= jnp.where(first, HALO, 0)) & (rid < jnp.where(last, HALO + tm, ext_rows))

    def project(j, half):
        lo = j * SEG + half * SUB
        if IN_GROUP_KINDS[j] in NEEDS_HALO:
            return jnp.dot(h_scr[...], w_ref[:, lo:lo + SUB], preferred_element_type=F32)
        return jnp.dot(h_scr[HALO:HALO + tm, :], w_ref[:, lo:lo + SUB],
                       preferred_element_type=F32)

    def slabs_of(half):
        return range(half * SUB // LANES, (half + 1) * SUB // LANES)

    def pool_epilogue(r, half):
        t = s * tm + lax.broadcasted_iota(jnp.int32, (tm, LANES), 0)
        for n_g, g in enumerate(slabs_of(half)):
            exa[g] = jnp.where(inside, r[:, n_g * LANES:(n_g + 1) * LANES], 0.0)
            w = POOL_WINDOWS[g]
            before, after = w // 2, w - w // 2 - 1
            total = exa[g, HALO - before:HALO - before + tm, :]
            for dlt in range(-before + 1, after + 1):
                total = total + exa[g, HALO + dlt:HALO + dlt + tm, :]
            lo = jnp.clip(t - before, 0, seq_len - 1)
            hi = jnp.clip(t + after, 0, seq_len - 1)
            cnt = (hi - lo + 1).astype(F32)
            o_ref[:, g * LANES:(g + 1) * LANES] = (
                total / cnt - exa[g, HALO:HALO + tm, :]).astype(BF16)

    def glu_epilogue(r, half):
        for n_g, g in enumerate(slabs_of(half)):
            gate = _sigmoid(r[:, n_g * LANES:(n_g + 1) * LANES])
            exy[g] = jnp.where(inside, exy[g] * gate, 0.0)

    def conv_piece(c, g, anchor):
        base = c * CONV_CHUNK + HALO - CONV_K // 2
        cols = slice(g * LANES, (g + 1) * LANES)
        acc = dwb_ref[:, cols]
        for k in range(CONV_K):
            tap = jnp.tile(dw_ref[k:k + 1, cols] + anchor, (CONV_CHUNK // 8, 1))
            acc = acc + exy[g, base + k:base + k + CONV_CHUNK, :] * tap
        ycv[c * CONV_CHUNK:(c + 1) * CONV_CHUNK, cols] = acc

    def exact_zero_from(r):
        bits = pltpu.bitcast(r[0:8, 0:LANES], jnp.uint32)
        half = jnp.uint32(16)
        gone = lax.shift_right_logical(lax.shift_right_logical(bits, half), half)
        return pltpu.bitcast(gone, F32)

    def norm_swish():
        yc = ycv[...]
        mu = jnp.mean(yc, axis=-1, keepdims=True)
        dev = yc - mu
        var = jnp.mean(dev * dev, axis=-1, keepdims=True)
        yn = dev * lax.rsqrt(var + LN_EPS) * lng_ref[...] + lnb_ref[...]
        o_ref[:, 4 * SEG:5 * SEG] = _silu(yn).astype(BF16)

    units = [(j, half) for j in IN_GROUP_ORDER for half in range(SEG // SUB)]
    n_units = len(units)
    pieces = []
    pieces_per_unit = -(-(tm // CONV_CHUNK) * SLABS // (n_units - 8))
    conv_done = False

    pending = project(*units[0])
    for n, (j, half) in enumerate(units):
        r = pending
        if n + 1 < n_units:
            pending = project(*units[n + 1])
        kind = IN_GROUP_KINDS[j]
        if kind == _GLU_A:
            for n_g, g in enumerate(slabs_of(half)):
                exy[g] = r[:, n_g * LANES:(n_g + 1) * LANES]
        elif kind == _GLU_B:
            glu_epilogue(r, half)
            pieces += [(c, g) for c in range(tm // CONV_CHUNK) for g in slabs_of(half)]
        elif kind == _POOL:
            pool_epilogue(r, half)
        else:
            if kind == _PLAIN:
                out = r
            elif kind == _SILU:
                out = _silu(r)
            elif kind == _ROPE_Q:
                out = _rope(r, cq_ref, sq_ref)
            else:
                out = _rope(r, ck_ref, sk_ref)
            where, slot = OUT_SLOT[j]
            out = out.astype(BF16)
            if where == "p":
                o_ref[:, slot * SEG + half * SUB:slot * SEG + (half + 1) * SUB] = out
            else:
                for n_g, g in enumerate(slabs_of(half)):
                    qkv_ref[slot + g] = out[:, n_g * LANES:(n_g + 1) * LANES]
        if kind not in (_GLU_A, _GLU_B):
            if pieces:
                anchor = exact_zero_from(r)
                for c, g in pieces[:pieces_per_unit]:
                    conv_piece(c, g, anchor)
                pieces = pieces[pieces_per_unit:]
            elif not conv_done:
                norm_swish()
                conv_done = True
    assert conv_done and not pieces


def _in_call(x2d, layer, mod, mod_row, norm_g, w_in, tabs, conv_dw, conv_dw_b, ln_g, ln_b,
             seq_len, tm):
    n_tok, d = x2d.shape
    seq_tiles = seq_len // tm
    halo_blocks = tm // HALO
    n_halo = n_tok // HALO
    if mod_row is None:
        row = lambda i: i // seq_tiles
    else:
        row = lambda i: mod_row
    tab_spec = pl.BlockSpec((tm, LANES), lambda i: (i % seq_tiles, 0))
    kern = functools.partial(_in_kernel, tm=tm, seq_tiles=seq_tiles, seq_len=seq_len)
    return pl.pallas_call(
        kern,
        grid=(n_tok // tm,),
        in_specs=[
            pl.BlockSpec((tm, d), lambda i: (i, 0)),
            pl.BlockSpec((HALO, d), lambda i: (jnp.maximum(i * halo_blocks - 1, 0), 0)),
            pl.BlockSpec((HALO, d), lambda i: (jnp.minimum((i + 1) * halo_blocks, n_halo - 1), 0)),
            pl.BlockSpec((None, None, 1, d), lambda i: (layer, row(i), 0, 1)),
            pl.BlockSpec((None, None, 1, d), lambda i: (layer, row(i), 0, 0)),
            _layer_spec(norm_g, layer, 1),
            _layer_spec(w_in, layer, 1),
            tab_spec, tab_spec, tab_spec, tab_spec,
            _layer_spec(conv_dw, layer, 1), _layer_spec(conv_dw_b, layer, 1),
            _layer_spec(ln_g, layer, 1), _layer_spec(ln_b, layer, 1),
        ],
        out_specs=[pl.BlockSpec((tm, N_P_GROUPS * SEG), lambda i: (i, 0)),
                   pl.BlockSpec((N_QKV_BLOCKS, tm, LANES), lambda i: (0, i, 0))],
        out_shape=[jax.ShapeDtypeStruct((n_tok, N_P_GROUPS * SEG), BF16),
                   jax.ShapeDtypeStruct((N_QKV_BLOCKS, n_tok, LANES), BF16)],
        scratch_shapes=[pltpu.VMEM((tm + 2 * HALO, d), BF16),
                        pltpu.VMEM((SLABS, tm + 2 * HALO, LANES), F32),
                        pltpu.VMEM((SLABS, tm + 2 * HALO, LANES), F32),
                        pltpu.VMEM((tm, SEG), F32)],
        compiler_params=pltpu.CompilerParams(
            dimension_semantics=("parallel",), vmem_limit_bytes=VMEM_LIMIT_BYTES),
        name="in_proj",
    )(x2d, x2d, x2d, mod, mod, norm_g, w_in, *tabs, conv_dw, conv_dw_b, ln_g, ln_b)


def _edge_class_rows(rows):
    n_dr = 2 * NA_KH - 1
    idx = np.full((N_EDGE_CLASSES, 2, KEY_ROWS), n_dr, np.int32)
    for c, r0 in enumerate((0, 2, 4, rows - 4, rows - 2)):
        ks = int(np.clip(r0 - NA_KH // 2, 0, rows - KEY_ROWS))
        for i in range(2):
            r = r0 + i
            rs = int(np.clip(r - NA_KH // 2, 0, rows - NA_KH))
            for jr in range(KEY_ROWS):
                kr = ks + jr
                if rs <= kr < rs + NA_KH:
                    idx[c, i, jr] = kr - r + NA_KH - 1
    return idx


def _att_kernel(q_ref, k_ref, v_ref, kc_ref, vc_ref, bd_ref, o_ref, tab, *, rows):
    rb = pl.program_id(2)
    pair_q = 2 * GRID_W
    n_pairs = ATT_ROWS_PER_STEP // 2
    nt = (((1,), (1,)), ((), ()))

    @pl.when(rb == 0)
    def _():
        idx = _edge_class_rows(rows)
        low = lax.broadcasted_iota(jnp.int32, (GRID_W, LANES), 1) < GRID_W
        for c in range(N_EDGE_CLASSES):
            for hh in range(2):
                for i in range(2):
                    r_lo = (2 * hh + i) * GRID_W
                    for m in range(KEY_ROWS // 2):
                        a, b = int(idx[c, i, 2 * m]), int(idx[c, i, 2 * m + 1])
                        tab[c, r_lo:r_lo + GRID_W, m * LANES:(m + 1) * LANES] = LOG2E * jnp.where(
                            low, bd_ref[hh, a], bd_ref[hh, b])

    kc = kc_ref[...]
    vc = vc_ref[...]
    first = lax.broadcasted_iota(jnp.int32, (pair_q, LANES), 1) < HEAD_DIM

    def scores(pp):
        r0 = rb * ATT_ROWS_PER_STEP + 2 * pp
        ks = jnp.clip(r0 - NA_KH // 2, 0, rows - KEY_ROWS)
        start = pl.multiple_of(ks * GRID_W, LANES)
        cls = jnp.where(r0 == 0, 0, jnp.where(r0 == 2, 1, jnp.where(
            r0 == rows - 4, 3, jnp.where(r0 == rows - 2, 4, 2))))
        q2 = q_ref[pp * pair_q:(pp + 1) * pair_q, :]
        zero = jnp.zeros_like(q2)
        qs = jnp.concatenate([jnp.where(first, q2, zero), jnp.where(first, zero, q2)], axis=0)
        kw = k_ref[pl.ds(start, KEY_ROWS * GRID_W), :]
        s_loc = lax.dot_general(qs, kw, nt, preferred_element_type=F32) + tab[cls]
        s_ctx = lax.dot_general(qs, kc, nt, preferred_element_type=F32)
        return s_loc, s_ctx, start

    def softmax(s_loc, s_ctx):
        m = jnp.maximum(s_loc.max(axis=-1, keepdims=True), s_ctx.max(axis=-1, keepdims=True))
        e_loc = jnp.exp2(s_loc - m)
        e_ctx = jnp.exp2(s_ctx - m)
        denom = e_loc.sum(axis=-1, keepdims=True) + e_ctx.sum(axis=-1, keepdims=True)
        return e_loc.astype(BF16), e_ctx.astype(BF16), denom

    def weighted_values(pp, e_loc, e_ctx, denom, start):
        vw = v_ref[pl.ds(start, KEY_ROWS * GRID_W), :]
        o = (jnp.dot(e_loc, vw, preferred_element_type=F32)
             + jnp.dot(e_ctx, vc, preferred_element_type=F32)) / denom
        out = jnp.where(first, o[:pair_q], o[pair_q:])
        o_ref[pp * pair_q:(pp + 1) * pair_q, :] = out.astype(BF16)

    pending = scores(0)
    for pp in range(n_pairs):
        s_loc, s_ctx, start = pending
        if pp + 1 < n_pairs:
            pending = scores(pp + 1)
        weighted_values(pp, *softmax(s_loc, s_ctx), start)


def _att_call(qkv_lat, qkv_ctx, layer, bias_blocks, batch, seq_len, ctx_len):
    n_tok = qkv_lat.shape[1]
    rows = seq_len // GRID_W
    heads2 = (2 * SEG) // LANES
    rb_per_seq = rows // ATT_ROWS_PER_STEP
    tq = ATT_ROWS_PER_STEP * GRID_W
    n_dr = bias_blocks.shape[2]
    return pl.pallas_call(
        functools.partial(_att_kernel, rows=rows),
        grid=(batch, heads2, rb_per_seq),
        in_specs=[
            pl.BlockSpec((None, tq, LANES), lambda b, h, r: (Q_BLOCK + h, b * rb_per_seq + r, 0)),
            pl.BlockSpec((None, seq_len, LANES), lambda b, h, r: (K_BLOCK + h, b, 0)),
            pl.BlockSpec((None, seq_len, LANES), lambda b, h, r: (V_BLOCK + h, b, 0)),
            pl.BlockSpec((None, ctx_len, LANES), lambda b, h, r: (K_BLOCK + h, b, 0)),
            pl.BlockSpec((None, ctx_len, LANES), lambda b, h, r: (V_BLOCK + h, b, 0)),
            pl.BlockSpec((None, 2, n_dr, GRID_W, LANES), lambda b, h, r: (layer, h, 0, 0, 0)),
        ],
        out_specs=pl.BlockSpec((None, tq, LANES), lambda b, h, r: (h, b * rb_per_seq + r, 0)),
        out_shape=jax.ShapeDtypeStruct((heads2, n_tok, LANES), BF16),
        scratch_shapes=[pltpu.VMEM((N_EDGE_CLASSES, 4 * GRID_W, KEY_ROWS * GRID_W), F32)],
        compiler_params=pltpu.CompilerParams(
            dimension_semantics=("arbitrary", "arbitrary", "arbitrary"),
            vmem_limit_bytes=VMEM_LIMIT_BYTES),
        name="nbr_attention",
    )(qkv_lat, qkv_lat, qkv_lat, qkv_ctx, qkv_ctx, bias_blocks)


def _two_head_attention(q2, k2, v2):
    m_rows = q2.shape[0]
    first = lax.broadcasted_iota(jnp.int32, q2.shape, 1) < HEAD_DIM
    zero = jnp.zeros_like(q2)
    qs = jnp.concatenate([jnp.where(first, q2, zero), jnp.where(first, zero, q2)], axis=0)
    s = lax.dot_general(qs, k2, (((1,), (1,)), ((), ())), preferred_element_type=F32)
    e = jnp.exp2(s - s.max(axis=-1, keepdims=True))
    o = jnp.dot(e.astype(BF16), v2, preferred_element_type=F32) / e.sum(axis=-1, keepdims=True)
    return jnp.where(first, o[:m_rows], o[m_rows:])


def _ctx_att_kernel(q_ref, k_ref, v_ref, o_ref):
    o_ref[...] = _two_head_attention(q_ref[...], k_ref[...], v_ref[...]).astype(BF16)


def _ctx_att_call(qkv_ctx, batch, ctx_len):
    n_tok = qkv_ctx.shape[1]
    heads2 = (2 * SEG) // LANES
    return pl.pallas_call(
        _ctx_att_kernel,
        grid=(batch, heads2),
        in_specs=[
            pl.BlockSpec((None, ctx_len, LANES), lambda b, h: (Q_BLOCK + h, b, 0)),
            pl.BlockSpec((None, ctx_len, LANES), lambda b, h: (K_BLOCK + h, b, 0)),
            pl.BlockSpec((None, ctx_len, LANES), lambda b, h: (V_BLOCK + h, b, 0)),
        ],
        out_specs=pl.BlockSpec((None, ctx_len, LANES), lambda b, h: (h, b, 0)),
        out_shape=jax.ShapeDtypeStruct((heads2, n_tok, LANES), BF16),
        compiler_params=pltpu.CompilerParams(
            dimension_semantics=("parallel", "parallel"), vmem_limit_bytes=VMEM_LIMIT_BYTES),
        name="ctx_attention",
    )(qkv_ctx, qkv_ctx, qkv_ctx)


def _bias_blocks(na_rpb):
    depth, heads, n_dr, n_dc = na_rpb.shape
    cq = np.arange(GRID_W)[:, None]
    ck = np.arange(GRID_W)[None, :]
    cs = np.clip(cq - NA_KW // 2, 0, GRID_W - NA_KW)
    col_ok = (ck >= cs) & (ck < cs + NA_KW)
    dc = ck - cq + NA_KW - 1
    onehot = ((np.arange(n_dc)[:, None, None] == dc[None]) & col_ok[None]).astype(np.float32)
    b1 = jnp.einsum("lhrd,dqk->lhrqk", na_rpb.astype(F32), jnp.asarray(onehot),
                    precision=lax.Precision.HIGHEST)
    b1 = jnp.where(col_ok, b1, MASK_VALUE)
    masked = jnp.full((depth, heads, 1, GRID_W, GRID_W), MASK_VALUE, F32)
    b1 = jnp.concatenate([b1, masked], axis=2)
    return jnp.concatenate([b1, b1], axis=-1)


def _mix_kernel(x_ref, gate_ref, pooled_ref, ag_ref, at_ref, bg_ref, sw_ref, cg_ref,
                wpool_ref, ps_ref, pw_ref, pwb_ref, wout_ref, fg_ref, o_ref, *, final):
    ya = jnp.dot(pooled_ref[...], wpool_ref[...], preferred_element_type=F32)
    ya = ya * ps_ref[...] * ag_ref[...].astype(F32)
    cm = jnp.dot(sw_ref[...], pw_ref[...], preferred_element_type=F32) + pwb_ref[...]
    ycc = cm * cg_ref[...].astype(F32)
    att = jnp.concatenate([at_ref[hp] for hp in range(at_ref.shape[0])], axis=1)
    yb = att * bg_ref[...]
    cat = jnp.concatenate([ya.astype(BF16), yb, ycc.astype(BF16)], axis=1)
    mixed = jnp.dot(cat, wout_ref[...], preferred_element_type=F32)
    xn = x_ref[...] + gate_ref[...] * mixed
    if final:
        ms = jnp.mean(xn * xn, axis=-1, keepdims=True)
        xn = xn * lax.rsqrt(ms + NORM_EPS) * fg_ref[...]
    o_ref[...] = xn


def _mix_call(x2d, layer, mod, mod_row, p, att, wpool, pool_scale, conv_pw, conv_pw_b, w_out,
              final_g, seq_len, tm, final):
    n_tok, d = x2d.shape
    seq_tiles = seq_len // tm
    if mod_row is None:
        row = lambda i: i // seq_tiles
    else:
        row = lambda i: mod_row

    def group(colblk):
        return pl.BlockSpec((tm, SEG), lambda i: (i, colblk))

    return pl.pallas_call(
        functools.partial(_mix_kernel, final=final),
        grid=(n_tok // tm,),
        in_specs=[
            pl.BlockSpec((tm, d), lambda i: (i, 0)),
            pl.BlockSpec((None, None, 1, d), lambda i: (layer, row(i), 0, 2)),
            group(0),
            group(1),
            pl.BlockSpec((att.shape[0], tm, LANES), lambda i: (0, i, 0)),
            pl.BlockSpec((tm, 2 * SEG), lambda i: (i, 1)),
            group(4),
            group(5),
            _layer_spec(wpool, layer, 1), _layer_spec(pool_scale, layer, 1),
            _layer_spec(conv_pw, layer, 1), _layer_spec(conv_pw_b, layer, 1),
            _layer_spec(w_out, layer, 1),
            pl.BlockSpec(final_g.shape, lambda i: (0, 0), pipeline_mode=pl.Buffered(1)),
        ],
        out_specs=pl.BlockSpec((tm, d), lambda i: (i, 0)),
        out_shape=jax.ShapeDtypeStruct((n_tok, d), F32),
        compiler_params=pltpu.CompilerParams(
            dimension_semantics=("parallel",), vmem_limit_bytes=VMEM_LIMIT_BYTES),
        name="mix_out",
    )(x2d, mod, p, p, att, p, p, p, wpool, pool_scale, conv_pw, conv_pw_b, w_out, final_g)


def _rope_tables(seq_len):
    half = HEAD_DIM // 2
    freqs = ROPE_THETA ** (-jnp.arange(0, half, 2, dtype=F32) / half)
    t = jnp.arange(seq_len)
    ang_r = (t // GRID_W).astype(F32)[:, None] * freqs[None, :]
    ang_c = (t % GRID_W).astype(F32)[:, None] * freqs[None, :]
    cos = jnp.concatenate([jnp.cos(ang_r)] * 2 + [jnp.cos(ang_c)] * 2, axis=-1)
    sin = jnp.concatenate([-jnp.sin(ang_r), jnp.sin(ang_r), -jnp.sin(ang_c), jnp.sin(ang_c)], axis=-1)
    cos = jnp.tile(cos, (1, LANES // HEAD_DIM))
    sin = jnp.tile(sin, (1, LANES // HEAD_DIM))
    return cos * Q_SCALE, sin * Q_SCALE, cos, sin


def kernel(x, c, ctx, c_ctx, w_mod, b_mod, norm_g, w_in, w_pool, pool_scale, na_rpb, conv_dw,
           conv_dw_b, conv_ln_g, conv_ln_b, conv_pw, conv_pw_b, w_out, final_norm_g):
    batch, seq_len, d = x.shape
    ctx_len = ctx.shape[1]
    depth = w_mod.shape[0]
    assert d == 4 * SEG and w_in.shape[2] == N_IN_GROUPS * SEG
    assert seq_len % (ATT_ROWS_PER_STEP * GRID_W) == 0 and seq_len // GRID_W >= 12
    assert batch + 1 <= 8 and w_pool.shape[-1] == LANES

    cstack = jnp.zeros((8, d), F32).at[:batch].set(c).at[batch].set(c_ctx)
    mod = _mod_call(cstack, w_mod, b_mod).reshape(depth, 8, 1, 3 * d)

    lat_tabs = _rope_tables(seq_len)
    ones = jnp.ones((ctx_len, LANES), F32)
    ctx_tabs = (ones * Q_SCALE, jnp.zeros_like(ones), ones, jnp.zeros_like(ones))
    bias_blocks = _bias_blocks(na_rpb)

    groups = len(POOL_WINDOWS)
    pg = w_pool.shape[-1]
    eye = jnp.eye(groups, dtype=F32)
    wpool_bd = (eye[None, :, None, :, None] * w_pool[:, :, :, None, :]).reshape(
        depth, groups * pg, groups * pg).astype(BF16)
    w_in_b = w_in.astype(BF16)
    w_out_b = w_out.astype(BF16)
    conv_pw_b16 = conv_pw.astype(BF16)
    row_vec = lambda a: a.reshape(depth, 1, a.shape[-1])
    in_w = (row_vec(norm_g), w_in_b)
    conv_w = (conv_dw, row_vec(conv_dw_b), row_vec(conv_ln_g), row_vec(conv_ln_b))
    mix_w = (wpool_bd, row_vec(pool_scale), conv_pw_b16, row_vec(conv_pw_b), w_out_b,
             final_norm_g.reshape(1, d))

    tm_in = min(256, seq_len)
    tm_mix = min(512, seq_len)
    xl = x.reshape(batch * seq_len, d)
    xc = ctx.reshape(batch * ctx_len, d)
    for i in range(depth):
        last = i == depth - 1
        p_ctx, qkv_ctx = _in_call(xc, i, mod, batch, *in_w, ctx_tabs, *conv_w, ctx_len,
                                  min(tm_in, ctx_len))
        p_lat, qkv_lat = _in_call(xl, i, mod, None, *in_w, lat_tabs, *conv_w, seq_len, tm_in)
        att = _att_call(qkv_lat, qkv_ctx, i, bias_blocks, batch, seq_len, ctx_len)
        xl = _mix_call(xl, i, mod, None, p_lat, att, *mix_w, seq_len, tm_mix, last)
        if not last:
            att_c = _ctx_att_call(qkv_ctx, batch, ctx_len)
            xc = _mix_call(xc, i, mod, batch, p_ctx, att_c, *mix_w, ctx_len, ctx_len, False)
    return xl.reshape(batch, seq_len, d)
```

```python
import functools

import numpy as np
import jax
import jax.numpy as jnp
from jax import lax
from jax.experimental import pallas as pl
from jax.experimental.pallas import tpu as pltpu

F32 = jnp.float32
BF16 = jnp.bfloat16

GRID_W = 64
POOL_WINDOWS = (2, 4, 8, 16)
HEAD_DIM = 64
NA_KH = 8
NA_KW = 16
ROPE_THETA = 10000.0
CONV_K = 31
NORM_EPS = 1e-6
LN_EPS = 1e-5

LANES = 128
VMEM_LIMIT_BYTES = 56 * 1024 * 1024

SEG = 512
SLABS = SEG // LANES
SUB = 256
HALO = 16
MASK_VALUE = -1e30
KEY_ROWS = 10
N_EDGE_CLASSES = 5
ATT_ROWS_PER_STEP = 64
LOG2E = 1.4426950408889634
Q_SCALE = HEAD_DIM ** -0.5 * LOG2E
CONV_CHUNK = 32


def _sigmoid(v):
    return 1.0 / (1.0 + jnp.exp(-v))


def _silu(v):
    return v * _sigmoid(v)


def _layer_spec(a, layer, grid_rank):
    zeros = (0,) * (a.ndim - 1)
    if grid_rank == 1:
        index_map = lambda i: (layer,) + zeros
    else:
        index_map = lambda i, j: (layer,) + zeros
    return pl.BlockSpec((None,) + a.shape[1:], index_map, pipeline_mode=pl.Buffered(1))


def _mod_kernel(c_ref, w_ref, b_ref, o_ref):
    a = _silu(c_ref[...]).astype(BF16)
    o_ref[0] = jnp.dot(a, w_ref[0].astype(BF16), preferred_element_type=F32) + b_ref[0]


def _mod_call(cstack, w_mod, b_mod):
    depth, d, n = w_mod.shape
    tn = 768
    return pl.pallas_call(
        _mod_kernel,
        grid=(depth, n // tn),
        in_specs=[
            pl.BlockSpec((8, d), lambda l, j: (0, 0)),
            pl.BlockSpec((1, d, tn), lambda l, j: (l, 0, j)),
            pl.BlockSpec((1, 1, tn), lambda l, j: (l, 0, j)),
        ],
        out_specs=pl.BlockSpec((1, 8, tn), lambda l, j: (l, 0, j)),
        out_shape=jax.ShapeDtypeStruct((depth, 8, n), F32),
        compiler_params=pltpu.CompilerParams(
            dimension_semantics=("parallel", "parallel"), vmem_limit_bytes=VMEM_LIMIT_BYTES),
        name="adaln_mod",
    )(cstack, w_mod, b_mod.reshape(depth, 1, n))


_POOL, _SILU, _ROPE_Q, _ROPE_K, _PLAIN, _GLU_A, _GLU_B = range(7)
IN_GROUP_KINDS = (_POOL, _SILU, _ROPE_Q, _ROPE_Q, _ROPE_K, _ROPE_K, _PLAIN, _PLAIN,
                  _SILU, _SILU, _GLU_A, _GLU_B, _SILU)
N_IN_GROUPS = len(IN_GROUP_KINDS)
OUT_SLOT = {0: ("p", 0), 1: ("p", 1), 2: ("qkv", 0), 3: ("qkv", 4), 4: ("qkv", 8), 5: ("qkv", 12),
            6: ("qkv", 16), 7: ("qkv", 20), 8: ("p", 2), 9: ("p", 3), 11: ("p", 4), 12: ("p", 5)}
N_P_GROUPS = 6
N_QKV_BLOCKS = 24
Q_BLOCK, K_BLOCK, V_BLOCK = 0, 8, 16
IN_GROUP_ORDER = (10, 11) + tuple(range(10)) + (12,)
NEEDS_HALO = (_POOL, _GLU_A, _GLU_B)


def _rope(r, cos_ref, sin_ref):
    n = r.shape[1]
    reps = n // LANES
    lane = lax.broadcasted_iota(jnp.int32, r.shape, 1)
    up = pltpu.roll(r, n - 16, axis=1)
    dn = pltpu.roll(r, 16, axis=1)
    partner = jnp.where((lane % 32) < 16, up, dn)
    cos = jnp.tile(cos_ref[...], (1, reps))
    sin = jnp.tile(sin_ref[...], (1, reps))
    return r * cos + partner * sin


def _in_kernel(x_ref, xp_ref, xn_ref, scale_ref, shift_ref, g_ref, w_ref,
               cq_ref, sq_ref, ck_ref, sk_ref, dw_ref, dwb_ref, lng_ref, lnb_ref,
               o_ref, qkv_ref, h_scr, exa, exy, ycv, *, tm, seq_tiles, seq_len):
    s = pl.program_id(0) % seq_tiles
    first = s == 0
    last = s == seq_tiles - 1
    ext_rows = tm + 2 * HALO

    gain = g_ref[...] * (1.0 + scale_ref[...])
    shift = shift_ref[...]

    def normed(xv):
        ms = jnp.mean(xv * xv, axis=-1, keepdims=True)
        return (xv * lax.rsqrt(ms + NORM_EPS) * gain + shift).astype(BF16)

    h_scr[0:HALO, :] = normed(xp_ref[...])
    h_scr[HALO:HALO + tm, :] = normed(x_ref[...])
    h_scr[HALO + tm:, :] = normed(xn_ref[...])

    rid = lax.broadcasted_iota(jnp.int32, (ext_rows, LANES), 0)
    inside = (rid >= jnp.where(first, HALO, 0)) & (rid < jnp.where(last, HALO + tm, ext_rows))

    def project(j, half):
        lo = j * SEG + half * SUB
        if IN_GROUP_KINDS[j] in NEEDS_HALO:
            return jnp.dot(h_scr[...], w_ref[:, lo:lo + SUB], preferred_element_type=F32)
        return jnp.dot(h_scr[HALO:HALO + tm, :], w_ref[:, lo:lo + SUB],
                       preferred_element_type=F32)

    def slabs_of(half):
        return range(half * SUB // LANES, (half + 1) * SUB // LANES)

    def pool_epilogue(r, half):
        t = s * tm + lax.broadcasted_iota(jnp.int32, (tm, LANES), 0)
        for n_g, g in enumerate(slabs_of(half)):
            exa[g] = jnp.where(inside, r[:, n_g * LANES:(n_g + 1) * LANES], 0.0)
            w = POOL_WINDOWS[g]
            before, after = w // 2, w - w // 2 - 1
            total = exa[g, HALO - before:HALO - before + tm, :]
            for dlt in range(-before + 1, after + 1):
                total = total + exa[g, HALO + dlt:HALO + dlt + tm, :]
            lo = jnp.clip(t - before, 0, seq_len - 1)
            hi = jnp.clip(t + after, 0, seq_len - 1)
            cnt = (hi - lo + 1).astype(F32)
            o_ref[:, g * LANES:(g + 1) * LANES] = (
                total / cnt - exa[g, HALO:HALO + tm, :]).astype(BF16)

    def glu_epilogue(r, half):
        for n_g, g in enumerate(slabs_of(half)):
            gate = _sigmoid(r[:, n_g * LANES:(n_g + 1) * LANES])
            exy[g] = jnp.where(inside, exy[g] * gate, 0.0)

    def conv_piece(c, g, anchor):
        base = c * CONV_CHUNK + HALO - CONV_K // 2
        cols = slice(g * LANES, (g + 1) * LANES)
        acc = dwb_ref[:, cols]
        for k in range(CONV_K):
            tap = jnp.tile(dw_ref[k:k + 1, cols] + anchor, (CONV_CHUNK // 8, 1))
            acc = acc + exy[g, base + k:base + k + CONV_CHUNK, :] * tap
        ycv[c * CONV_CHUNK:(c + 1) * CONV_CHUNK, cols] = acc

    def exact_zero_from(r):
        bits = pltpu.bitcast(r[0:8, 0:LANES], jnp.uint32)
        half = jnp.uint32(16)
        gone = lax.shift_right_logical(lax.shift_right_logical(bits, half), half)
        return pltpu.bitcast(gone, F32)

    def norm_swish():
        yc = ycv[...]
        mu = jnp.mean(yc, axis=-1, keepdims=True)
        dev = yc - mu
        var = jnp.mean(dev * dev, axis=-1, keepdims=True)
        yn = dev * lax.rsqrt(var + LN_EPS) * lng_ref[...] + lnb_ref[...]
        o_ref[:, 4 * SEG:5 * SEG] = _silu(yn).astype(BF16)

    units = [(j, half) for j in IN_GROUP_ORDER for half in range(SEG // SUB)]
    n_units = len(units)
    pieces = []
    pieces_per_unit = -(-(tm // CONV_CHUNK) * SLABS // (n_units - 8))
    conv_done = False

    pending = project(*units[0])
    for n, (j, half) in enumerate(units):
        r = pending
        if n + 1 < n_units:
            pending = project(*units[n + 1])
        kind = IN_GROUP_KINDS[j]
        if kind == _GLU_A:
            for n_g, g in enumerate(slabs_of(half)):
                exy[g] = r[:, n_g * LANES:(n_g + 1) * LANES]
        elif kind == _GLU_B:
            glu_epilogue(r, half)
            pieces += [(c, g) for c in range(tm // CONV_CHUNK) for g in slabs_of(half)]
        elif kind == _POOL:
            pool_epilogue(r, half)
        else:
            if kind == _PLAIN:
                out = r
            elif kind == _SILU:
                out = _silu(r)
            elif kind == _ROPE_Q:
                out = _rope(r, cq_ref, sq_ref)
            else:
                out = _rope(r, ck_ref, sk_ref)
            where, slot = OUT_SLOT[j]
            out = out.astype(BF16)
            if where == "p":
                o_ref[:, slot * SEG + half * SUB:slot * SEG + (half + 1) * SUB] = out
            else:
                for n_g, g in enumerate(slabs_of(half)):
                    qkv_ref[slot + g] = out[:, n_g * LANES:(n_g + 1) * LANES]
        if kind not in (_GLU_A, _GLU_B):
            if pieces:
                anchor = exact_zero_from(r)
                for c, g in pieces[:pieces_per_unit]:
                    conv_piece(c, g, anchor)
                pieces = pieces[pieces_per_unit:]
            elif not conv_done:
                norm_swish()
                conv_done = True
    assert conv_done and not pieces


def _in_call(x2d, layer, mod, mod_row, norm_g, w_in, tabs, conv_dw, conv_dw_b, ln_g, ln_b,
             seq_len, tm):
    n_tok, d = x2d.shape
    seq_tiles = seq_len // tm
    halo_blocks = tm // HALO
    n_halo = n_tok // HALO
    if mod_row is None:
        row = lambda i: i // seq_tiles
    else:
        row = lambda i: mod_row
    tab_spec = pl.BlockSpec((tm, LANES), lambda i: (i % seq_tiles, 0))
    kern = functools.partial(_in_kernel, tm=tm, seq_tiles=seq_tiles, seq_len=seq_len)
    return pl.pallas_call(
        kern,
        grid=(n_tok // tm,),
        in_specs=[
            pl.BlockSpec((tm, d), lambda i: (i, 0)),
            pl.BlockSpec((HALO, d), lambda i: (jnp.maximum(i * halo_blocks - 1, 0), 0)),
            pl.BlockSpec((HALO, d), lambda i: (jnp.minimum((i + 1) * halo_blocks, n_halo - 1), 0)),
            pl.BlockSpec((None, None, 1, d), lambda i: (layer, row(i), 0, 1)),
            pl.BlockSpec((None, None, 1, d), lambda i: (layer, row(i), 0, 0)),
            _layer_spec(norm_g, layer, 1),
            _layer_spec(w_in, layer, 1),
            tab_spec, tab_spec, tab_spec, tab_spec,
            _layer_spec(conv_dw, layer, 1), _layer_spec(conv_dw_b, layer, 1),
            _layer_spec(ln_g, layer, 1), _layer_spec(ln_b, layer, 1),
        ],
        out_specs=[pl.BlockSpec((tm, N_P_GROUPS * SEG), lambda i: (i, 0)),
                   pl.BlockSpec((N_QKV_BLOCKS, tm, LANES), lambda i: (0, i, 0))],
        out_shape=[jax.ShapeDtypeStruct((n_tok, N_P_GROUPS * SEG), BF16),
                   jax.ShapeDtypeStruct((N_QKV_BLOCKS, n_tok, LANES), BF16)],
        scratch_shapes=[pltpu.VMEM((tm + 2 * HALO, d), BF16),
                        pltpu.VMEM((SLABS, tm + 2 * HALO, LANES), F32),
                        pltpu.VMEM((SLABS, tm + 2 * HALO, LANES), F32),
                        pltpu.VMEM((tm, SEG), F32)],
        compiler_params=pltpu.CompilerParams(
            dimension_semantics=("parallel",), vmem_limit_bytes=VMEM_LIMIT_BYTES),
        name="in_proj",
    )(x2d, x2d, x2d, mod, mod, norm_g, w_in, *tabs, conv_dw, conv_dw_b, ln_g, ln_b)


def _edge_class_rows(rows):
    n_dr = 2 * NA_KH - 1
    idx = np.full((N_EDGE_CLASSES, 2, KEY_ROWS), n_dr, np.int32)
    for c, r0 in enumerate((0, 2, 4, rows - 4, rows - 2)):
        ks = int(np.clip(r0 - NA_KH // 2, 0, rows - KEY_ROWS))
        for i in range(2):
            r = r0 + i
            rs = int(np.clip(r - NA_KH // 2, 0, rows - NA_KH))
            for jr in range(KEY_ROWS):
                kr = ks + jr
                if rs <= kr < rs + NA_KH:
                    idx[c, i, jr] = kr - r + NA_KH - 1
    return idx


def _att_kernel(q_ref, k_ref, v_ref, kc_ref, vc_ref, bd_ref, o_ref, tab, *, rows):
    rb = pl.program_id(2)
    pair_q = 2 * GRID_W
    n_pairs = ATT_ROWS_PER_STEP // 2
    nt = (((1,), (1,)), ((), ()))

    @pl.when(rb == 0)
    def _():
        idx = _edge_class_rows(rows)
        low = lax.broadcasted_iota(jnp.int32, (GRID_W, LANES), 1) < GRID_W
        for c in range(N_EDGE_CLASSES):
            for hh in range(2):
                for i in range(2):
                    r_lo = (2 * hh + i) * GRID_W
                    for m in range(KEY_ROWS // 2):
                        a, b = int(idx[c, i, 2 * m]), int(idx[c, i, 2 * m + 1])
                        tab[c, r_lo:r_lo + GRID_W, m * LANES:(m + 1) * LANES] = LOG2E * jnp.where(
                            low, bd_ref[hh, a], bd_ref[hh, b])

    kc = kc_ref[...]
    vc = vc_ref[...]
    first = lax.broadcasted_iota(jnp.int32, (pair_q, LANES), 1) < HEAD_DIM

    def scores(pp):
        r0 = rb * ATT_ROWS_PER_STEP + 2 * pp
        ks = jnp.clip(r0 - NA_KH // 2, 0, rows - KEY_ROWS)
        start = pl.multiple_of(ks * GRID_W, LANES)
        cls = jnp.where(r0 == 0, 0, jnp.where(r0 == 2, 1, jnp.where(
            r0 == rows - 4, 3, jnp.where(r0 == rows - 2, 4, 2))))
        q2 = q_ref[pp * pair_q:(pp + 1) * pair_q, :]
        zero = jnp.zeros_like(q2)
        qs = jnp.concatenate([jnp.where(first, q2, zero), jnp.where(first, zero, q2)], axis=0)
        kw = k_ref[pl.ds(start, KEY_ROWS * GRID_W), :]
        s_loc = lax.dot_general(qs, kw, nt, preferred_element_type=F32) + tab[cls]
        s_ctx = lax.dot_general(qs, kc, nt, preferred_element_type=F32)
        return s_loc, s_ctx, start

    def softmax(s_loc, s_ctx):
        m = jnp.maximum(s_loc.max(axis=-1, keepdims=True), s_ctx.max(axis=-1, keepdims=True))
        e_loc = jnp.exp2(s_loc - m)
        e_ctx = jnp.exp2(s_ctx - m)
        denom = e_loc.sum(axis=-1, keepdims=True) + e_ctx.sum(axis=-1, keepdims=True)
        return e_loc.astype(BF16), e_ctx.astype(BF16), denom

    def weighted_values(pp, e_loc, e_ctx, denom, start):
        vw = v_ref[pl.ds(start, KEY_ROWS * GRID_W), :]
        o = (jnp.dot(e_loc, vw, preferred_element_type=F32)
             + jnp.dot(e_ctx, vc, preferred_element_type=F32)) / denom
        out = jnp.where(first, o[:pair_q], o[pair_q:])
        o_ref[pp * pair_q:(pp + 1) * pair_q, :] = out.astype(BF16)

    pending = scores(0)
    for pp in range(n_pairs):
        s_loc, s_ctx, start = pending
        if pp + 1 < n_pairs:
            pending = scores(pp + 1)
        weighted_values(pp, *softmax(s_loc, s_ctx), start)


def _att_call(qkv_lat, qkv_ctx, layer, bias_blocks, batch, seq_len, ctx_len):
    n_tok = qkv_lat.shape[1]
    rows = seq_len // GRID_W
    heads2 = (2 * SEG) // LANES
    rb_per_seq = rows // ATT_ROWS_PER_STEP
    tq = ATT_ROWS_PER_STEP * GRID_W
    n_dr = bias_blocks.shape[2]
    return pl.pallas_call(
        functools.partial(_att_kernel, rows=rows),
        grid=(batch, heads2, rb_per_seq),
        in_specs=[
            pl.BlockSpec((None, tq, LANES), lambda b, h, r: (Q_BLOCK + h, b * rb_per_seq + r, 0)),
            pl.BlockSpec((None, seq_len, LANES), lambda b, h, r: (K_BLOCK + h, b, 0)),
            pl.BlockSpec((None, seq_len, LANES), lambda b, h, r: (V_BLOCK + h, b, 0)),
            pl.BlockSpec((None, ctx_len, LANES), lambda b, h, r: (K_BLOCK + h, b, 0)),
            pl.BlockSpec((None, ctx_len, LANES), lambda b, h, r: (V_BLOCK + h, b, 0)),
            pl.BlockSpec((None, 2, n_dr, GRID_W, LANES), lambda b, h, r: (layer, h, 0, 0, 0)),
        ],
        out_specs=pl.BlockSpec((None, tq, LANES), lambda b, h, r: (h, b * rb_per_seq + r, 0)),
        out_shape=jax.ShapeDtypeStruct((heads2, n_tok, LANES), BF16),
        scratch_shapes=[pltpu.VMEM((N_EDGE_CLASSES, 4 * GRID_W, KEY_ROWS * GRID_W), F32)],
        compiler_params=pltpu.CompilerParams(
            dimension_semantics=("arbitrary", "arbitrary", "arbitrary"),
            vmem_limit_bytes=VMEM_LIMIT_BYTES),
        name="nbr_attention",
    )(qkv_lat, qkv_lat, qkv_lat, qkv_ctx, qkv_ctx, bias_blocks)


def _two_head_attention(q2, k2, v2):
    m_rows = q2.shape[0]
    first = lax.broadcasted_iota(jnp.int32, q2.shape, 1) < HEAD_DIM
    zero = jnp.zeros_like(q2)
    qs = jnp.concatenate([jnp.where(first, q2, zero), jnp.where(first, zero, q2)], axis=0)
    s = lax.dot_general(qs, k2, (((1,), (1,)), ((), ())), preferred_element_type=F32)
    e = jnp.exp2(s - s.max(axis=-1, keepdims=True))
    o = jnp.dot(e.astype(BF16), v2, preferred_element_type=F32) / e.sum(axis=-1, keepdims=True)
    return jnp.where(first, o[:m_rows], o[m_rows:])


def _ctx_att_kernel(q_ref, k_ref, v_ref, o_ref):
    o_ref[...] = _two_head_attention(q_ref[...], k_ref[...], v_ref[...]).astype(BF16)


def _ctx_att_call(qkv_ctx, batch, ctx_len):
    n_tok = qkv_ctx.shape[1]
    heads2 = (2 * SEG) // LANES
    return pl.pallas_call(
        _ctx_att_kernel,
        grid=(batch, heads2),
        in_specs=[
            pl.BlockSpec((None, ctx_len, LANES), lambda b, h: (Q_BLOCK + h, b, 0)),
            pl.BlockSpec((None, ctx_len, LANES), lambda b, h: (K_BLOCK + h, b, 0)),
            pl.BlockSpec((None, ctx_len, LANES), lambda b, h: (V_BLOCK + h, b, 0)),
        ],
        out_specs=pl.BlockSpec((None, ctx_len, LANES), lambda b, h: (h, b, 0)),
        out_shape=jax.ShapeDtypeStruct((heads2, n_tok, LANES), BF16),
        compiler_params=pltpu.CompilerParams(
            dimension_semantics=("parallel", "parallel"), vmem_limit_bytes=VMEM_LIMIT_BYTES),
        name="ctx_attention",
    )(qkv_ctx, qkv_ctx, qkv_ctx)


def _bias_blocks(na_rpb):
    depth, heads, n_dr, n_dc = na_rpb.shape
    cq = np.arange(GRID_W)[:, None]
    ck = np.arange(GRID_W)[None, :]
    cs = np.clip(cq - NA_KW // 2, 0, GRID_W - NA_KW)
    col_ok = (ck >= cs) & (ck < cs + NA_KW)
    dc = ck - cq + NA_KW - 1
    onehot = ((np.arange(n_dc)[:, None, None] == dc[None]) & col_ok[None]).astype(np.float32)
    onehot = np.concatenate([onehot, onehot], axis=-1)
    mask = np.where(np.concatenate([col_ok, col_ok], axis=-1), 0.0, MASK_VALUE).astype(np.float32)
    mask = np.concatenate([np.broadcast_to(mask, (n_dr,) + mask.shape),
                           np.full((1,) + mask.shape, MASK_VALUE, np.float32)], axis=0)
    rpb = jnp.concatenate([na_rpb.astype(F32), jnp.zeros((depth, heads, 1, n_dc), F32)], axis=2)
    return jnp.einsum("lhrd,dqk->lhrqk", rpb, jnp.asarray(onehot),
                      precision=lax.Precision.HIGHEST) + jnp.asarray(mask)


def _mix_kernel(x_ref, gate_ref, pooled_ref, ag_ref, at_ref, bg_ref, sw_ref, cg_ref,
                wpool_ref, ps_ref, pw_ref, pwb_ref, wout_ref, fg_ref, o_ref, *, final):
    ya = jnp.dot(pooled_ref[...], wpool_ref[...], preferred_element_type=F32)
    ya = ya * ps_ref[...] * ag_ref[...].astype(F32)
    cm = jnp.dot(sw_ref[...], pw_ref[...], preferred_element_type=F32) + pwb_ref[...]
    ycc = cm * cg_ref[...].astype(F32)
    att = jnp.concatenate([at_ref[hp] for hp in range(at_ref.shape[0])], axis=1)
    yb = att * bg_ref[...]
    cat = jnp.concatenate([ya.astype(BF16), yb, ycc.astype(BF16)], axis=1)
    mixed = jnp.dot(cat, wout_ref[...], preferred_element_type=F32)
    xn = x_ref[...] + gate_ref[...] * mixed
    if final:
        ms = jnp.mean(xn * xn, axis=-1, keepdims=True)
        xn = xn * lax.rsqrt(ms + NORM_EPS) * fg_ref[...]
    o_ref[...] = xn


def _mix_call(x2d, layer, mod, mod_row, p, att, wpool, pool_scale, conv_pw, conv_pw_b, w_out,
              final_g, seq_len, tm, final):
    n_tok, d = x2d.shape
    seq_tiles = seq_len // tm
    if mod_row is None:
        row = lambda i: i // seq_tiles
    else:
        row = lambda i: mod_row

    def group(colblk):
        return pl.BlockSpec((tm, SEG), lambda i: (i, colblk))

    return pl.pallas_call(
        functools.partial(_mix_kernel, final=final),
        grid=(n_tok // tm,),
        in_specs=[
            pl.BlockSpec((tm, d), lambda i: (i, 0)),
            pl.BlockSpec((None, None, 1, d), lambda i: (layer, row(i), 0, 2)),
            group(0),
            group(1),
            pl.BlockSpec((att.shape[0], tm, LANES), lambda i: (0, i, 0)),
            pl.BlockSpec((tm, 2 * SEG), lambda i: (i, 1)),
            group(4),
            group(5),
            _layer_spec(wpool, layer, 1), _layer_spec(pool_scale, layer, 1),
            _layer_spec(conv_pw, layer, 1), _layer_spec(conv_pw_b, layer, 1),
            _layer_spec(w_out, layer, 1),
            pl.BlockSpec(final_g.shape, lambda i: (0, 0), pipeline_mode=pl.Buffered(1)),
        ],
        out_specs=pl.BlockSpec((tm, d), lambda i: (i, 0)),
        out_shape=jax.ShapeDtypeStruct((n_tok, d), F32),
        compiler_params=pltpu.CompilerParams(
            dimension_semantics=("parallel",), vmem_limit_bytes=VMEM_LIMIT_BYTES),
        name="mix_out",
    )(x2d, mod, p, p, att, p, p, p, wpool, pool_scale, conv_pw, conv_pw_b, w_out, final_g)


def _rope_tables(seq_len):
    half = HEAD_DIM // 2
    freqs = ROPE_THETA ** (-jnp.arange(0, half, 2, dtype=F32) / half)
    t = jnp.arange(seq_len)
    ang_r = (t // GRID_W).astype(F32)[:, None] * freqs[None, :]
    ang_c = (t % GRID_W).astype(F32)[:, None] * freqs[None, :]
    cos = jnp.concatenate([jnp.cos(ang_r)] * 2 + [jnp.cos(ang_c)] * 2, axis=-1)
    sin = jnp.concatenate([-jnp.sin(ang_r), jnp.sin(ang_r), -jnp.sin(ang_c), jnp.sin(ang_c)], axis=-1)
    cos = jnp.tile(cos, (1, LANES // HEAD_DIM))
    sin = jnp.tile(sin, (1, LANES // HEAD_DIM))
    return cos * Q_SCALE, sin * Q_SCALE, cos, sin


def kernel(x, c, ctx, c_ctx, w_mod, b_mod, norm_g, w_in, w_pool, pool_scale, na_rpb, conv_dw,
           conv_dw_b, conv_ln_g, conv_ln_b, conv_pw, conv_pw_b, w_out, final_norm_g):
    batch, seq_len, d = x.shape
    ctx_len = ctx.shape[1]
    depth = w_mod.shape[0]
    assert d == 4 * SEG and w_in.shape[2] == N_IN_GROUPS * SEG
    assert seq_len % (ATT_ROWS_PER_STEP * GRID_W) == 0 and seq_len // GRID_W >= 12
    assert batch + 1 <= 8 and w_pool.shape[-1] == LANES

    cstack = jnp.zeros((8, d), F32).at[:batch].set(c).at[batch].set(c_ctx)
    mod = _mod_call(cstack, w_mod, b_mod).reshape(depth, 8, 1, 3 * d)

    lat_tabs = _rope_tables(seq_len)
    ones = jnp.ones((ctx_len, LANES), F32)
    ctx_tabs = (ones * Q_SCALE, jnp.zeros_like(ones), ones, jnp.zeros_like(ones))
    bias_blocks = _bias_blocks(na_rpb)

    groups = len(POOL_WINDOWS)
    pg = w_pool.shape[-1]
    eye = jnp.eye(groups, dtype=F32)
    wpool_bd = (eye[None, :, None, :, None] * w_pool[:, :, :, None, :]).reshape(
        depth, groups * pg, groups * pg).astype(BF16)
    w_in_b = w_in.astype(BF16)
    w_out_b = w_out.astype(BF16)
    conv_pw_b16 = conv_pw.astype(BF16)
    row_vec = lambda a: a.reshape(depth, 1, a.shape[-1])
    in_w = (row_vec(norm_g), w_in_b)
    conv_w = (conv_dw, row_vec(conv_dw_b), row_vec(conv_ln_g), row_vec(conv_ln_b))
    mix_w = (wpool_bd, row_vec(pool_scale), conv_pw_b16, row_vec(conv_pw_b), w_out_b,
             final_norm_g.reshape(1, d))

    tm_in = min(256, seq_len)
    tm_mix = min(512, seq_len)
    xl = x.reshape(batch * seq_len, d)
    xc = ctx.reshape(batch * ctx_len, d)
    for i in range(depth):
        last = i == depth - 1
        p_ctx, qkv_ctx = _in_call(xc, i, mod, batch, *in_w, ctx_tabs, *conv_w, ctx_len,
                                  min(tm_in, ctx_len))
        p_lat, qkv_lat = _in_call(xl, i, mod, None, *in_w, lat_tabs, *conv_w, seq_len, tm_in)
        att = _att_call(qkv_lat, qkv_ctx, i, bias_blocks, batch, seq_len, ctx_len)
        xl = _mix_call(xl, i, mod, None, p_lat, att, *mix_w, seq_len, tm_mix, last)
        if not last:
            att_c = _ctx_att_call(qkv_ctx, batch, ctx_len)
            xc = _mix_call(xc, i, mod, batch, p_ctx, att_c, *mix_w, ctx_len, ctx_len, False)
    return xl.reshape(batch, seq_len, d)
```

```python
import functools

import numpy as np
import jax
import jax.numpy as jnp
from jax import lax
from jax.experimental import pallas as pl
from jax.experimental.pallas import tpu as pltpu

F32 = jnp.float32
BF16 = jnp.bfloat16

GRID_W = 64
POOL_WINDOWS = (2, 4, 8, 16)
HEAD_DIM = 64
NA_KH = 8
NA_KW = 16
ROPE_THETA = 10000.0
CONV_K = 31
NORM_EPS = 1e-6
LN_EPS = 1e-5

LANES = 128
VMEM_LIMIT_BYTES = 56 * 1024 * 1024

IN_TOKEN_TILE = 256
MIX_TOKEN_TILE = 512
MOD_COLUMN_TILE = 768

SEG = 512
SLABS = SEG // LANES
SUB = 256
HALO = 16
MASK_VALUE = -1e30
KEY_ROWS = 10
N_EDGE_CLASSES = 5
ATT_ROWS_PER_STEP = 64
LOG2E = 1.4426950408889634
Q_SCALE = HEAD_DIM ** -0.5 * LOG2E
CONV_CHUNK = 32


def _sigmoid(v):
    return 1.0 / (1.0 + jnp.exp(-v))


def _silu(v):
    return v * _sigmoid(v)


def _layer_spec(a, layer, grid_rank):
    zeros = (0,) * (a.ndim - 1)
    if grid_rank == 1:
        index_map = lambda i: (layer,) + zeros
    else:
        index_map = lambda i, j: (layer,) + zeros
    return pl.BlockSpec((None,) + a.shape[1:], index_map, pipeline_mode=pl.Buffered(1))


def _mod_kernel(c_ref, w_ref, b_ref, o_ref):
    a = _silu(c_ref[...]).astype(BF16)
    o_ref[0] = jnp.dot(a, w_ref[0].astype(BF16), preferred_element_type=F32) + b_ref[0]


def _mod_call(cstack, w_mod, b_mod):
    depth, d, n = w_mod.shape
    tn = MOD_COLUMN_TILE
    return pl.pallas_call(
        _mod_kernel,
        grid=(depth, n // tn),
        in_specs=[
            pl.BlockSpec((8, d), lambda l, j: (0, 0)),
            pl.BlockSpec((1, d, tn), lambda l, j: (l, 0, j)),
            pl.BlockSpec((1, 1, tn), lambda l, j: (l, 0, j)),
        ],
        out_specs=pl.BlockSpec((1, 8, tn), lambda l, j: (l, 0, j)),
        out_shape=jax.ShapeDtypeStruct((depth, 8, n), F32),
        compiler_params=pltpu.CompilerParams(
            dimension_semantics=("parallel", "parallel"), vmem_limit_bytes=VMEM_LIMIT_BYTES),
        name="adaln_mod",
    )(cstack, w_mod, b_mod.reshape(depth, 1, n))


_POOL, _SILU, _ROPE_Q, _ROPE_K, _PLAIN, _GLU_A, _GLU_B = range(7)
IN_GROUP_KINDS = (_POOL, _SILU, _ROPE_Q, _ROPE_Q, _ROPE_K, _ROPE_K, _PLAIN, _PLAIN,
                  _SILU, _SILU, _GLU_A, _GLU_B, _SILU)
N_IN_GROUPS = len(IN_GROUP_KINDS)
OUT_SLOT = {0: ("p", 0), 1: ("p", 1), 2: ("qkv", 0), 3: ("qkv", 4), 4: ("qkv", 8), 5: ("qkv", 12),
            6: ("qkv", 16), 7: ("qkv", 20), 8: ("p", 2), 9: ("p", 3), 11: ("p", 4), 12: ("p", 5)}
N_P_GROUPS = 6
N_QKV_BLOCKS = 24
Q_BLOCK, K_BLOCK, V_BLOCK = 0, 8, 16
IN_GROUP_ORDER = (10, 11) + tuple(range(10)) + (12,)
NEEDS_HALO = (_POOL, _GLU_A, _GLU_B)


def _rope(r, cos_ref, sin_ref):
    n = r.shape[1]
    reps = n // LANES
    lane = lax.broadcasted_iota(jnp.int32, r.shape, 1)
    up = pltpu.roll(r, n - 16, axis=1)
    dn = pltpu.roll(r, 16, axis=1)
    partner = jnp.where((lane % 32) < 16, up, dn)
    cos = jnp.tile(cos_ref[...], (1, reps))
    sin = jnp.tile(sin_ref[...], (1, reps))
    return r * cos + partner * sin


def _in_kernel(x_ref, xp_ref, xn_ref, scale_ref, shift_ref, g_ref, w_ref,
               cos_ref, sin_ref, dw_ref, dwb_ref, lng_ref, lnb_ref,
               o_ref, qkv_ref, h_scr, exa, exy, ycv, *, tm, seq_tiles, seq_len):
    s = pl.program_id(0) % seq_tiles
    first = s == 0
    last = s == seq_tiles - 1
    ext_rows = tm + 2 * HALO

    gain = g_ref[...] * (1.0 + scale_ref[...])
    shift = shift_ref[...]

    def normed(xv):
        ms = jnp.mean(xv * xv, axis=-1, keepdims=True)
        return (xv * lax.rsqrt(ms + NORM_EPS) * gain + shift).astype(BF16)

    h_scr[0:HALO, :] = normed(xp_ref[...])
    h_scr[HALO:HALO + tm, :] = normed(x_ref[...])
    h_scr[HALO + tm:, :] = normed(xn_ref[...])

    rid = lax.broadcasted_iota(jnp.int32, (ext_rows, LANES), 0)
    inside = (rid >= jnp.where(first, HALO, 0)) & (rid < jnp.where(last, HALO + tm, ext_rows))

    def project(j, half):
        lo = j * SEG + half * SUB
        if IN_GROUP_KINDS[j] in NEEDS_HALO:
            return jnp.dot(h_scr[...], w_ref[:, lo:lo + SUB], preferred_element_type=F32)
        return jnp.dot(h_scr[HALO:HALO + tm, :], w_ref[:, lo:lo + SUB],
                       preferred_element_type=F32)

    def slabs_of(half):
        return range(half * SUB // LANES, (half + 1) * SUB // LANES)

    def pool_epilogue(r, half):
        t = s * tm + lax.broadcasted_iota(jnp.int32, (tm, LANES), 0)
        for n_g, g in enumerate(slabs_of(half)):
            exa[g] = jnp.where(inside, r[:, n_g * LANES:(n_g + 1) * LANES], 0.0)
            w = POOL_WINDOWS[g]
            before, after = w // 2, w - w // 2 - 1
            total = exa[g, HALO - before:HALO - before + tm, :]
            for dlt in range(-before + 1, after + 1):
                total = total + exa[g, HALO + dlt:HALO + dlt + tm, :]
            lo = jnp.clip(t - before, 0, seq_len - 1)
            hi = jnp.clip(t + after, 0, seq_len - 1)
            cnt = (hi - lo + 1).astype(F32)
            o_ref[:, g * LANES:(g + 1) * LANES] = (
                total / cnt - exa[g, HALO:HALO + tm, :]).astype(BF16)

    def glu_epilogue(r, half):
        for n_g, g in enumerate(slabs_of(half)):
            gate = _sigmoid(r[:, n_g * LANES:(n_g + 1) * LANES])
            exy[g] = jnp.where(inside, exy[g] * gate, 0.0)

    def conv_piece(c, g, anchor):
        base = c * CONV_CHUNK + HALO - CONV_K // 2
        cols = slice(g * LANES, (g + 1) * LANES)
        acc = dwb_ref[:, cols]
        for k in range(CONV_K):
            tap = jnp.tile(dw_ref[k:k + 1, cols] + anchor, (CONV_CHUNK // 8, 1))
            acc = acc + exy[g, base + k:base + k + CONV_CHUNK, :] * tap
        ycv[c * CONV_CHUNK:(c + 1) * CONV_CHUNK, cols] = acc

    def exact_zero_from(r):
        bits = pltpu.bitcast(r[0:8, 0:LANES], jnp.uint32)
        half = jnp.uint32(16)
        gone = lax.shift_right_logical(lax.shift_right_logical(bits, half), half)
        return pltpu.bitcast(gone, F32)

    def norm_swish():
        yc = ycv[...]
        mu = jnp.mean(yc, axis=-1, keepdims=True)
        dev = yc - mu
        var = jnp.mean(dev * dev, axis=-1, keepdims=True)
        yn = dev * lax.rsqrt(var + LN_EPS) * lng_ref[...] + lnb_ref[...]
        o_ref[:, 4 * SEG:5 * SEG] = _silu(yn).astype(BF16)

    units = [(j, half) for j in IN_GROUP_ORDER for half in range(SEG // SUB)]
    n_units = len(units)
    pieces = []
    pieces_per_unit = -(-(tm // CONV_CHUNK) * SLABS // (n_units - 8))
    conv_done = False

    pending = project(*units[0])
    for n, (j, half) in enumerate(units):
        r = pending
        if n + 1 < n_units:
            pending = project(*units[n + 1])
        kind = IN_GROUP_KINDS[j]
        if kind == _GLU_A:
            for n_g, g in enumerate(slabs_of(half)):
                exy[g] = r[:, n_g * LANES:(n_g + 1) * LANES]
        elif kind == _GLU_B:
            glu_epilogue(r, half)
            pieces += [(c, g) for c in range(tm // CONV_CHUNK) for g in slabs_of(half)]
        elif kind == _POOL:
            pool_epilogue(r, half)
        else:
            if kind == _PLAIN:
                out = r
            elif kind == _SILU:
                out = _silu(r)
            elif kind == _ROPE_Q:
                out = _rope(r, cos_ref, sin_ref) * Q_SCALE
            else:
                out = _rope(r, cos_ref, sin_ref)
            where, slot = OUT_SLOT[j]
            out = out.astype(BF16)
            if where == "p":
                o_ref[:, slot * SEG + half * SUB:slot * SEG + (half + 1) * SUB] = out
            else:
                for n_g, g in enumerate(slabs_of(half)):
                    qkv_ref[slot + g] = out[:, n_g * LANES:(n_g + 1) * LANES]
        if kind not in (_GLU_A, _GLU_B):
            if pieces:
                anchor = exact_zero_from(r)
                for c, g in pieces[:pieces_per_unit]:
                    conv_piece(c, g, anchor)
                pieces = pieces[pieces_per_unit:]
            elif not conv_done:
                norm_swish()
                conv_done = True
    assert conv_done and not pieces


def _in_call(x2d, layer, mod, mod_row, norm_g, w_in, tabs, conv_dw, conv_dw_b, ln_g, ln_b,
             seq_len, tm):
    n_tok, d = x2d.shape
    seq_tiles = seq_len // tm
    halo_blocks = tm // HALO
    n_halo = n_tok // HALO
    if mod_row is None:
        row = lambda i: i // seq_tiles
    else:
        row = lambda i: mod_row
    tab_spec = pl.BlockSpec((tm, LANES), lambda i: (i % seq_tiles, 0))
    kern = functools.partial(_in_kernel, tm=tm, seq_tiles=seq_tiles, seq_len=seq_len)
    return pl.pallas_call(
        kern,
        grid=(n_tok // tm,),
        in_specs=[
            pl.BlockSpec((tm, d), lambda i: (i, 0)),
            pl.BlockSpec((HALO, d), lambda i: (jnp.maximum(i * halo_blocks - 1, 0), 0)),
            pl.BlockSpec((HALO, d), lambda i: (jnp.minimum((i + 1) * halo_blocks, n_halo - 1), 0)),
            pl.BlockSpec((None, None, 1, d), lambda i: (layer, row(i), 0, 1)),
            pl.BlockSpec((None, None, 1, d), lambda i: (layer, row(i), 0, 0)),
            _layer_spec(norm_g, layer, 1),
            _layer_spec(w_in, layer, 1),
            tab_spec, tab_spec,
            _layer_spec(conv_dw, layer, 1), _layer_spec(conv_dw_b, layer, 1),
            _layer_spec(ln_g, layer, 1), _layer_spec(ln_b, layer, 1),
        ],
        out_specs=[pl.BlockSpec((tm, N_P_GROUPS * SEG), lambda i: (i, 0)),
                   pl.BlockSpec((N_QKV_BLOCKS, tm, LANES), lambda i: (0, i, 0))],
        out_shape=[jax.ShapeDtypeStruct((n_tok, N_P_GROUPS * SEG), BF16),
                   jax.ShapeDtypeStruct((N_QKV_BLOCKS, n_tok, LANES), BF16)],
        scratch_shapes=[pltpu.VMEM((tm + 2 * HALO, d), BF16),
                        pltpu.VMEM((SLABS, tm + 2 * HALO, LANES), F32),
                        pltpu.VMEM((SLABS, tm + 2 * HALO, LANES), F32),
                        pltpu.VMEM((tm, SEG), F32)],
        compiler_params=pltpu.CompilerParams(
            dimension_semantics=("parallel",), vmem_limit_bytes=VMEM_LIMIT_BYTES),
        name="in_proj",
    )(x2d, x2d, x2d, mod, mod, norm_g, w_in, *tabs, conv_dw, conv_dw_b, ln_g, ln_b)


def _edge_class_rows(rows):
    n_dr = 2 * NA_KH - 1
    idx = np.full((N_EDGE_CLASSES, 2, KEY_ROWS), n_dr, np.int32)
    for c, r0 in enumerate((0, 2, 4, rows - 4, rows - 2)):
        ks = int(np.clip(r0 - NA_KH // 2, 0, rows - KEY_ROWS))
        for i in range(2):
            r = r0 + i
            rs = int(np.clip(r - NA_KH // 2, 0, rows - NA_KH))
            for jr in range(KEY_ROWS):
                kr = ks + jr
                if rs <= kr < rs + NA_KH:
                    idx[c, i, jr] = kr - r + NA_KH - 1
    return idx


def _att_kernel(q_ref, k_ref, v_ref, kc_ref, vc_ref, bd_ref, o_ref, tab, *, rows):
    rb = pl.program_id(2)
    pair_q = 2 * GRID_W
    n_pairs = ATT_ROWS_PER_STEP // 2
    nt = (((1,), (1,)), ((), ()))

    @pl.when(rb == 0)
    def _():
        idx = _edge_class_rows(rows)
        low = lax.broadcasted_iota(jnp.int32, (GRID_W, LANES), 1) < GRID_W
        for c in range(N_EDGE_CLASSES):
            for hh in range(2):
                for i in range(2):
                    r_lo = (2 * hh + i) * GRID_W
                    for m in range(KEY_ROWS // 2):
                        a, b = int(idx[c, i, 2 * m]), int(idx[c, i, 2 * m + 1])
                        tab[c, r_lo:r_lo + GRID_W, m * LANES:(m + 1) * LANES] = LOG2E * jnp.where(
                            low, bd_ref[hh, a], bd_ref[hh, b])

    kc = kc_ref[...]
    vc = vc_ref[...]
    first = lax.broadcasted_iota(jnp.int32, (pair_q, LANES), 1) < HEAD_DIM

    def scores(pp):
        r0 = rb * ATT_ROWS_PER_STEP + 2 * pp
        ks = jnp.clip(r0 - NA_KH // 2, 0, rows - KEY_ROWS)
        start = pl.multiple_of(ks * GRID_W, LANES)
        cls = jnp.where(r0 == 0, 0, jnp.where(r0 == 2, 1, jnp.where(
            r0 == rows - 4, 3, jnp.where(r0 == rows - 2, 4, 2))))
        q2 = q_ref[pp * pair_q:(pp + 1) * pair_q, :]
        zero = jnp.zeros_like(q2)
        qs = jnp.concatenate([jnp.where(first, q2, zero), jnp.where(first, zero, q2)], axis=0)
        kw = k_ref[pl.ds(start, KEY_ROWS * GRID_W), :]
        s_loc = lax.dot_general(qs, kw, nt, preferred_element_type=F32) + tab[cls]
        s_ctx = lax.dot_general(qs, kc, nt, preferred_element_type=F32)
        return s_loc, s_ctx, start

    def softmax(s_loc, s_ctx):
        m = jnp.maximum(s_loc.max(axis=-1, keepdims=True), s_ctx.max(axis=-1, keepdims=True))
        e_loc = jnp.exp2(s_loc - m)
        e_ctx = jnp.exp2(s_ctx - m)
        denom = e_loc.sum(axis=-1, keepdims=True) + e_ctx.sum(axis=-1, keepdims=True)
        return e_loc.astype(BF16), e_ctx.astype(BF16), denom

    def weighted_values(pp, e_loc, e_ctx, denom, start):
        vw = v_ref[pl.ds(start, KEY_ROWS * GRID_W), :]
        o = (jnp.dot(e_loc, vw, preferred_element_type=F32)
             + jnp.dot(e_ctx, vc, preferred_element_type=F32)) / denom
        out = jnp.where(first, o[:pair_q], o[pair_q:])
        o_ref[pp * pair_q:(pp + 1) * pair_q, :] = out.astype(BF16)

    pending = scores(0)
    for pp in range(n_pairs):
        s_loc, s_ctx, start = pending
        if pp + 1 < n_pairs:
            pending = scores(pp + 1)
        weighted_values(pp, *softmax(s_loc, s_ctx), start)


def _att_call(qkv_lat, qkv_ctx, layer, bias_blocks, batch, seq_len, ctx_len):
    n_tok = qkv_lat.shape[1]
    rows = seq_len // GRID_W
    heads2 = (2 * SEG) // LANES
    rb_per_seq = rows // ATT_ROWS_PER_STEP
    tq = ATT_ROWS_PER_STEP * GRID_W
    n_dr = bias_blocks.shape[2]
    return pl.pallas_call(
        functools.partial(_att_kernel, rows=rows),
        grid=(batch, heads2, rb_per_seq),
        in_specs=[
            pl.BlockSpec((None, tq, LANES), lambda b, h, r: (Q_BLOCK + h, b * rb_per_seq + r, 0)),
            pl.BlockSpec((None, seq_len, LANES), lambda b, h, r: (K_BLOCK + h, b, 0)),
            pl.BlockSpec((None, seq_len, LANES), lambda b, h, r: (V_BLOCK + h, b, 0)),
            pl.BlockSpec((None, ctx_len, LANES), lambda b, h, r: (K_BLOCK + h, b, 0)),
            pl.BlockSpec((None, ctx_len, LANES), lambda b, h, r: (V_BLOCK + h, b, 0)),
            pl.BlockSpec((None, 2, n_dr, GRID_W, LANES), lambda b, h, r: (layer, h, 0, 0, 0)),
        ],
        out_specs=pl.BlockSpec((None, tq, LANES), lambda b, h, r: (h, b * rb_per_seq + r, 0)),
        out_shape=jax.ShapeDtypeStruct((heads2, n_tok, LANES), BF16),
        scratch_shapes=[pltpu.VMEM((N_EDGE_CLASSES, 4 * GRID_W, KEY_ROWS * GRID_W), F32)],
        compiler_params=pltpu.CompilerParams(
            dimension_semantics=("arbitrary", "arbitrary", "arbitrary"),
            vmem_limit_bytes=VMEM_LIMIT_BYTES),
        name="nbr_attention",
    )(qkv_lat, qkv_lat, qkv_lat, qkv_ctx, qkv_ctx, bias_blocks)


def _two_head_attention(q2, k2, v2):
    m_rows = q2.shape[0]
    first = lax.broadcasted_iota(jnp.int32, q2.shape, 1) < HEAD_DIM
    zero = jnp.zeros_like(q2)
    qs = jnp.concatenate([jnp.where(first, q2, zero), jnp.where(first, zero, q2)], axis=0)
    s = lax.dot_general(qs, k2, (((1,), (1,)), ((), ())), preferred_element_type=F32)
    e = jnp.exp2(s - s.max(axis=-1, keepdims=True))
    o = jnp.dot(e.astype(BF16), v2, preferred_element_type=F32) / e.sum(axis=-1, keepdims=True)
    return jnp.where(first, o[:m_rows], o[m_rows:])


def _ctx_att_kernel(q_ref, k_ref, v_ref, o_ref):
    o_ref[...] = _two_head_attention(q_ref[...], k_ref[...], v_ref[...]).astype(BF16)


def _ctx_att_call(qkv_ctx, batch, ctx_len):
    n_tok = qkv_ctx.shape[1]
    heads2 = (2 * SEG) // LANES
    return pl.pallas_call(
        _ctx_att_kernel,
        grid=(batch, heads2),
        in_specs=[
            pl.BlockSpec((None, ctx_len, LANES), lambda b, h: (Q_BLOCK + h, b, 0)),
            pl.BlockSpec((None, ctx_len, LANES), lambda b, h: (K_BLOCK + h, b, 0)),
            pl.BlockSpec((None, ctx_len, LANES), lambda b, h: (V_BLOCK + h, b, 0)),
        ],
        out_specs=pl.BlockSpec((None, ctx_len, LANES), lambda b, h: (h, b, 0)),
        out_shape=jax.ShapeDtypeStruct((heads2, n_tok, LANES), BF16),
        compiler_params=pltpu.CompilerParams(
            dimension_semantics=("parallel", "parallel"), vmem_limit_bytes=VMEM_LIMIT_BYTES),
        name="ctx_attention",
    )(qkv_ctx, qkv_ctx, qkv_ctx)


def _bias_blocks(na_rpb):
    depth, heads, n_dr, n_dc = na_rpb.shape
    cq = np.arange(GRID_W)[:, None]
    ck = np.arange(GRID_W)[None, :]
    cs = np.clip(cq - NA_KW // 2, 0, GRID_W - NA_KW)
    col_ok = (ck >= cs) & (ck < cs + NA_KW)
    dc = ck - cq + NA_KW - 1
    onehot = ((np.arange(n_dc)[:, None, None] == dc[None]) & col_ok[None]).astype(np.float32)
    onehot = np.concatenate([onehot, onehot], axis=-1)
    mask = np.where(np.concatenate([col_ok, col_ok], axis=-1), 0.0, MASK_VALUE).astype(np.float32)
    mask = np.concatenate([np.broadcast_to(mask, (n_dr,) + mask.shape),
                           np.full((1,) + mask.shape, MASK_VALUE, np.float32)], axis=0)
    rpb = jnp.concatenate([na_rpb.astype(F32), jnp.zeros((depth, heads, 1, n_dc), F32)], axis=2)
    return jnp.einsum("lhrd,dqk->lhrqk", rpb, jnp.asarray(onehot),
                      precision=lax.Precision.HIGHEST) + jnp.asarray(mask)


def _mix_kernel(x_ref, gate_ref, pooled_ref, ag_ref, at_ref, bg_ref, sw_ref, cg_ref,
                wpool_ref, ps_ref, pw_ref, pwb_ref, wout_ref, fg_ref, o_ref, *, final):
    ya = jnp.dot(pooled_ref[...], wpool_ref[...], preferred_element_type=F32)
    ya = ya * ps_ref[...] * ag_ref[...].astype(F32)
    cm = jnp.dot(sw_ref[...], pw_ref[...], preferred_element_type=F32) + pwb_ref[...]
    ycc = cm * cg_ref[...].astype(F32)
    att = jnp.concatenate([at_ref[hp] for hp in range(at_ref.shape[0])], axis=1)
    yb = att * bg_ref[...]
    cat = jnp.concatenate([ya.astype(BF16), yb, ycc.astype(BF16)], axis=1)
    mixed = jnp.dot(cat, wout_ref[...], preferred_element_type=F32)
    xn = x_ref[...] + gate_ref[...] * mixed
    if final:
        ms = jnp.mean(xn * xn, axis=-1, keepdims=True)
        xn = xn * lax.rsqrt(ms + NORM_EPS) * fg_ref[...]
    o_ref[...] = xn


def _mix_call(x2d, layer, mod, mod_row, p, att, wpool, pool_scale, conv_pw, conv_pw_b, w_out,
              final_g, seq_len, tm, final):
    n_tok, d = x2d.shape
    seq_tiles = seq_len // tm
    if mod_row is None:
        row = lambda i: i // seq_tiles
    else:
        row = lambda i: mod_row

    def group(colblk):
        return pl.BlockSpec((tm, SEG), lambda i: (i, colblk))

    return pl.pallas_call(
        functools.partial(_mix_kernel, final=final),
        grid=(n_tok // tm,),
        in_specs=[
            pl.BlockSpec((tm, d), lambda i: (i, 0)),
            pl.BlockSpec((None, None, 1, d), lambda i: (layer, row(i), 0, 2)),
            group(0),
            group(1),
            pl.BlockSpec((att.shape[0], tm, LANES), lambda i: (0, i, 0)),
            pl.BlockSpec((tm, 2 * SEG), lambda i: (i, 1)),
            group(4),
            group(5),
            _layer_spec(wpool, layer, 1), _layer_spec(pool_scale, layer, 1),
            _layer_spec(conv_pw, layer, 1), _layer_spec(conv_pw_b, layer, 1),
            _layer_spec(w_out, layer, 1),
            pl.BlockSpec(final_g.shape, lambda i: (0, 0), pipeline_mode=pl.Buffered(1)),
        ],
        out_specs=pl.BlockSpec((tm, d), lambda i: (i, 0)),
        out_shape=jax.ShapeDtypeStruct((n_tok, d), F32),
        compiler_params=pltpu.CompilerParams(
            dimension_semantics=("parallel",), vmem_limit_bytes=VMEM_LIMIT_BYTES),
        name="mix_out",
    )(x2d, mod, p, p, att, p, p, p, wpool, pool_scale, conv_pw, conv_pw_b, w_out, final_g)


def _rope_tables(seq_len):
    half = HEAD_DIM // 2
    freqs = ROPE_THETA ** (-jnp.arange(0, half, 2, dtype=F32) / half)
    t = jnp.arange(seq_len)
    ang_r = (t // GRID_W).astype(F32)[:, None] * freqs[None, :]
    ang_c = (t % GRID_W).astype(F32)[:, None] * freqs[None, :]
    cos = jnp.concatenate([jnp.cos(ang_r)] * 2 + [jnp.cos(ang_c)] * 2, axis=-1)
    sin = jnp.concatenate([-jnp.sin(ang_r), jnp.sin(ang_r), -jnp.sin(ang_c), jnp.sin(ang_c)], axis=-1)
    cos = jnp.tile(cos, (1, LANES // HEAD_DIM))
    sin = jnp.tile(sin, (1, LANES // HEAD_DIM))
    return cos, sin


def kernel(x, c, ctx, c_ctx, w_mod, b_mod, norm_g, w_in, w_pool, pool_scale, na_rpb, conv_dw,
           conv_dw_b, conv_ln_g, conv_ln_b, conv_pw, conv_pw_b, w_out, final_norm_g):
    batch, seq_len, d = x.shape
    ctx_len = ctx.shape[1]
    depth = w_mod.shape[0]
    assert d == 4 * SEG and w_in.shape[2] == N_IN_GROUPS * SEG
    assert seq_len % (ATT_ROWS_PER_STEP * GRID_W) == 0 and seq_len // GRID_W >= 12
    assert batch + 1 <= 8 and w_pool.shape[-1] == LANES

    cstack = jnp.zeros((8, d), F32).at[:batch].set(c).at[batch].set(c_ctx)
    mod = _mod_call(cstack, w_mod, b_mod).reshape(depth, 8, 1, 3 * d)

    lat_tabs = _rope_tables(seq_len)
    ones = jnp.ones((ctx_len, LANES), F32)
    ctx_tabs = (ones, jnp.zeros_like(ones))
    bias_blocks = _bias_blocks(na_rpb)

    groups = len(POOL_WINDOWS)
    pg = w_pool.shape[-1]
    eye = jnp.eye(groups, dtype=F32)
    wpool_bd = (eye[None, :, None, :, None] * w_pool[:, :, :, None, :]).reshape(
        depth, groups * pg, groups * pg).astype(BF16)
    w_in_b = w_in.astype(BF16)
    w_out_b = w_out.astype(BF16)
    conv_pw_b16 = conv_pw.astype(BF16)
    row_vec = lambda a: a.reshape(depth, 1, a.shape[-1])
    in_w = (row_vec(norm_g), w_in_b)
    conv_w = (conv_dw, row_vec(conv_dw_b), row_vec(conv_ln_g), row_vec(conv_ln_b))
    mix_w = (wpool_bd, row_vec(pool_scale), conv_pw_b16, row_vec(conv_pw_b), w_out_b,
             final_norm_g.reshape(1, d))

    xl = x.reshape(batch * seq_len, d)
    xc = ctx.reshape(batch * ctx_len, d)
    for i in range(depth):
        last = i == depth - 1
        p_ctx, qkv_ctx = _in_call(xc, i, mod, batch, *in_w, ctx_tabs, *conv_w, ctx_len,
                                  min(IN_TOKEN_TILE, ctx_len))
        p_lat, qkv_lat = _in_call(xl, i, mod, None, *in_w, lat_tabs, *conv_w, seq_len,
                                  min(IN_TOKEN_TILE, seq_len))
        att = _att_call(qkv_lat, qkv_ctx, i, bias_blocks, batch, seq_len, ctx_len)
        xl = _mix_call(xl, i, mod, None, p_lat, att, *mix_w, seq_len,
                       min(MIX_TOKEN_TILE, seq_len), last)
        if not last:
            att_c = _ctx_att_call(qkv_ctx, batch, ctx_len)
            xc = _mix_call(xc, i, mod, batch, p_ctx, att_c, *mix_w, ctx_len,
                           min(MIX_TOKEN_TILE, ctx_len), False)
    return xl.reshape(batch, seq_len, d)
```

```python
import functools

import numpy as np
import jax
import jax.numpy as jnp
from jax import lax
from jax.experimental import pallas as pl
from jax.experimental.pallas import tpu as pltpu

F32 = jnp.float32
BF16 = jnp.bfloat16

GRID_W = 64
POOL_WINDOWS = (2, 4, 8, 16)
HEAD_DIM = 64
NA_KH = 8
NA_KW = 16
ROPE_THETA = 10000.0
CONV_K = 31
NORM_EPS = 1e-6
LN_EPS = 1e-5

LANES = 128
VMEM_LIMIT_BYTES = 56 * 1024 * 1024

SEG = 512
SLABS = SEG // LANES
SUB = 256
HALO = 16
MASK_VALUE = -1e30
KEY_ROWS = 10
N_EDGE_CLASSES = 5
ATT_ROWS_PER_STEP = 64
LOG2E = 1.4426950408889634
Q_SCALE = HEAD_DIM ** -0.5 * LOG2E
CONV_CHUNK = 32


def _sigmoid(v):
    return 1.0 / (1.0 + jnp.exp(-v))


def _silu(v):
    return v * _sigmoid(v)


def _layer_spec(a, layer, grid_rank):
    zeros = (0,) * (a.ndim - 1)
    if grid_rank == 1:
        index_map = lambda i: (layer,) + zeros
    else:
        index_map = lambda i, j: (layer,) + zeros
    return pl.BlockSpec((None,) + a.shape[1:], index_map, pipeline_mode=pl.Buffered(1))


def _mod_kernel(c_ref, w_ref, b_ref, o_ref):
    a = _silu(c_ref[...]).astype(BF16)
    o_ref[0] = jnp.dot(a, w_ref[0].astype(BF16), preferred_element_type=F32) + b_ref[0]


def _mod_call(cstack, w_mod, b_mod):
    depth, d, n = w_mod.shape
    tn = 768
    return pl.pallas_call(
        _mod_kernel,
        grid=(depth, n // tn),
        in_specs=[
            pl.BlockSpec((8, d), lambda l, j: (0, 0)),
            pl.BlockSpec((1, d, tn), lambda l, j: (l, 0, j)),
            pl.BlockSpec((1, 1, tn), lambda l, j: (l, 0, j)),
        ],
        out_specs=pl.BlockSpec((1, 8, tn), lambda l, j: (l, 0, j)),
        out_shape=jax.ShapeDtypeStruct((depth, 8, n), F32),
        compiler_params=pltpu.CompilerParams(
            dimension_semantics=("parallel", "parallel"), vmem_limit_bytes=VMEM_LIMIT_BYTES),
        name="adaln_mod",
    )(cstack, w_mod, b_mod.reshape(depth, 1, n))


_POOL, _SILU, _ROPE_Q, _ROPE_K, _PLAIN, _GLU_A, _GLU_B = range(7)
IN_GROUP_KINDS = (_POOL, _SILU, _ROPE_Q, _ROPE_Q, _ROPE_K, _ROPE_K, _PLAIN, _PLAIN,
                  _SILU, _SILU, _GLU_A, _GLU_B, _SILU)
N_IN_GROUPS = len(IN_GROUP_KINDS)
OUT_SLOT = {0: ("p", 0), 1: ("p", 1), 2: ("qkv", 0), 3: ("qkv", 4), 4: ("qkv", 8), 5: ("qkv", 12),
            6: ("qkv", 16), 7: ("qkv", 20), 8: ("p", 2), 9: ("p", 3), 11: ("p", 4), 12: ("p", 5)}
N_P_GROUPS = 6
N_QKV_BLOCKS = 24
Q_BLOCK, K_BLOCK, V_BLOCK = 0, 8, 16
IN_GROUP_ORDER = (10, 11) + tuple(range(10)) + (12,)
NEEDS_HALO = (_POOL, _GLU_A, _GLU_B)


def _rope(r, cos_ref, sin_ref):
    n = r.shape[1]
    reps = n // LANES
    lane = lax.broadcasted_iota(jnp.int32, r.shape, 1)
    up = pltpu.roll(r, n - 16, axis=1)
    dn = pltpu.roll(r, 16, axis=1)
    partner = jnp.where((lane % 32) < 16, up, dn)
    cos = jnp.tile(cos_ref[...], (1, reps))
    sin = jnp.tile(sin_ref[...], (1, reps))
    return r * cos + partner * sin


def _in_kernel(x_ref, xp_ref, xn_ref, scale_ref, shift_ref, g_ref, w_ref,
               cq_ref, sq_ref, ck_ref, sk_ref, dw_ref, dwb_ref, lng_ref, lnb_ref,
               o_ref, qkv_ref, h_scr, exa, exy, ycv, *, tm, seq_tiles, seq_len, prenormed):
    s = pl.program_id(0) % seq_tiles
    first = s == 0
    last = s == seq_tiles - 1
    ext_rows = tm + 2 * HALO

    gain = g_ref[...] * (1.0 + scale_ref[...])
    shift = shift_ref[...]

    def normed(xv):
        if prenormed:
            return xv
        ms = jnp.mean(xv * xv, axis=-1, keepdims=True)
        return (xv * lax.rsqrt(ms + NORM_EPS) * gain + shift).astype(BF16)

    h_scr[0:HALO, :] = normed(xp_ref[...])
    h_scr[HALO:HALO + tm, :] = normed(x_ref[...])
    h_scr[HALO + tm:, :] = normed(xn_ref[...])

    rid = lax.broadcasted_iota(jnp.int32, (ext_rows, LANES), 0)
    inside = (rid >= jnp.where(first, HALO, 0)) & (rid < jnp.where(last, HALO + tm, ext_rows))

    def project(j, half):
        lo = j * SEG + half * SUB
        if IN_GROUP_KINDS[j] in NEEDS_HALO:
            return jnp.dot(h_scr[...], w_ref[:, lo:lo + SUB], preferred_element_type=F32)
        return jnp.dot(h_scr[HALO:HALO + tm, :], w_ref[:, lo:lo + SUB],
                       preferred_element_type=F32)

    def slabs_of(half):
        return range(half * SUB // LANES, (half + 1) * SUB // LANES)

    def pool_epilogue(r, half):
        t = s * tm + lax.broadcasted_iota(jnp.int32, (tm, LANES), 0)
        for n_g, g in enumerate(slabs_of(half)):
            exa[g] = jnp.where(inside, r[:, n_g * LANES:(n_g + 1) * LANES], 0.0)
            w = POOL_WINDOWS[g]
            before, after = w // 2, w - w // 2 - 1
            total = exa[g, HALO - before:HALO - before + tm, :]
            for dlt in range(-before + 1, after + 1):
                total = total + exa[g, HALO + dlt:HALO + dlt + tm, :]
            lo = jnp.clip(t - before, 0, seq_len - 1)
            hi = jnp.clip(t + after, 0, seq_len - 1)
            cnt = (hi - lo + 1).astype(F32)
            o_ref[:, g * LANES:(g + 1) * LANES] = (
                total / cnt - exa[g, HALO:HALO + tm, :]).astype(BF16)

    def glu_epilogue(r, half):
        for n_g, g in enumerate(slabs_of(half)):
            gate = _sigmoid(r[:, n_g * LANES:(n_g + 1) * LANES])
            exy[g] = jnp.where(inside, exy[g] * gate, 0.0)

    def conv_piece(c, g, anchor):
        base = c * CONV_CHUNK + HALO - CONV_K // 2
        cols = slice(g * LANES, (g + 1) * LANES)
        acc = dwb_ref[:, cols]
        for k in range(CONV_K):
            tap = jnp.tile(dw_ref[k:k + 1, cols] + anchor, (CONV_CHUNK // 8, 1))
            acc = acc + exy[g, base + k:base + k + CONV_CHUNK, :] * tap
        ycv[c * CONV_CHUNK:(c + 1) * CONV_CHUNK, cols] = acc

    def exact_zero_from(r):
        bits = pltpu.bitcast(r[0:8, 0:LANES], jnp.uint32)
        half = jnp.uint32(16)
        gone = lax.shift_right_logical(lax.shift_right_logical(bits, half), half)
        return pltpu.bitcast(gone, F32)

    def norm_swish():
        yc = ycv[...]
        mu = jnp.mean(yc, axis=-1, keepdims=True)
        dev = yc - mu
        var = jnp.mean(dev * dev, axis=-1, keepdims=True)
        yn = dev * lax.rsqrt(var + LN_EPS) * lng_ref[...] + lnb_ref[...]
        o_ref[:, 4 * SEG:5 * SEG] = _silu(yn).astype(BF16)

    units = [(j, half) for j in IN_GROUP_ORDER for half in range(SEG // SUB)]
    n_units = len(units)
    pieces = []
    pieces_per_unit = -(-(tm // CONV_CHUNK) * SLABS // (n_units - 8))
    conv_done = False

    pending = project(*units[0])
    for n, (j, half) in enumerate(units):
        r = pending
        if n + 1 < n_units:
            pending = project(*units[n + 1])
        kind = IN_GROUP_KINDS[j]
        if kind == _GLU_A:
            for n_g, g in enumerate(slabs_of(half)):
                exy[g] = r[:, n_g * LANES:(n_g + 1) * LANES]
        elif kind == _GLU_B:
            glu_epilogue(r, half)
            pieces += [(c, g) for c in range(tm // CONV_CHUNK) for g in slabs_of(half)]
        elif kind == _POOL:
            pool_epilogue(r, half)
        else:
            if kind == _PLAIN:
                out = r
            elif kind == _SILU:
                out = _silu(r)
            elif kind == _ROPE_Q:
                out = _rope(r, cq_ref, sq_ref)
            else:
                out = _rope(r, ck_ref, sk_ref)
            where, slot = OUT_SLOT[j]
            out = out.astype(BF16)
            if where == "p":
                o_ref[:, slot * SEG + half * SUB:slot * SEG + (half + 1) * SUB] = out
            else:
                for n_g, g in enumerate(slabs_of(half)):
                    qkv_ref[slot + g] = out[:, n_g * LANES:(n_g + 1) * LANES]
        if kind not in (_GLU_A, _GLU_B):
            if pieces:
                anchor = exact_zero_from(r)
                for c, g in pieces[:pieces_per_unit]:
                    conv_piece(c, g, anchor)
                pieces = pieces[pieces_per_unit:]
            elif not conv_done:
                norm_swish()
                conv_done = True
    assert conv_done and not pieces


def _in_call(x2d, layer, mod, mod_row, norm_g, w_in, tabs, conv_dw, conv_dw_b, ln_g, ln_b,
             seq_len, tm, prenormed=False):
    n_tok, d = x2d.shape
    seq_tiles = seq_len // tm
    halo_blocks = tm // HALO
    n_halo = n_tok // HALO
    if mod_row is None:
        row = lambda i: i // seq_tiles
    else:
        row = lambda i: mod_row
    tab_spec = pl.BlockSpec((tm, LANES), lambda i: (i % seq_tiles, 0))
    kern = functools.partial(_in_kernel, tm=tm, seq_tiles=seq_tiles, seq_len=seq_len,
                             prenormed=prenormed)
    return pl.pallas_call(
        kern,
        grid=(n_tok // tm,),
        in_specs=[
            pl.BlockSpec((tm, d), lambda i: (i, 0)),
            pl.BlockSpec((HALO, d), lambda i: (jnp.maximum(i * halo_blocks - 1, 0), 0)),
            pl.BlockSpec((HALO, d), lambda i: (jnp.minimum((i + 1) * halo_blocks, n_halo - 1), 0)),
            pl.BlockSpec((None, None, 1, d), lambda i: (layer, row(i), 0, 1)),
            pl.BlockSpec((None, None, 1, d), lambda i: (layer, row(i), 0, 0)),
            _layer_spec(norm_g, layer, 1),
            _layer_spec(w_in, layer, 1),
            tab_spec, tab_spec, tab_spec, tab_spec,
            _layer_spec(conv_dw, layer, 1), _layer_spec(conv_dw_b, layer, 1),
            _layer_spec(ln_g, layer, 1), _layer_spec(ln_b, layer, 1),
        ],
        out_specs=[pl.BlockSpec((tm, N_P_GROUPS * SEG), lambda i: (i, 0)),
                   pl.BlockSpec((N_QKV_BLOCKS, tm, LANES), lambda i: (0, i, 0))],
        out_shape=[jax.ShapeDtypeStruct((n_tok, N_P_GROUPS * SEG), BF16),
                   jax.ShapeDtypeStruct((N_QKV_BLOCKS, n_tok, LANES), BF16)],
        scratch_shapes=[pltpu.VMEM((tm + 2 * HALO, d), BF16),
                        pltpu.VMEM((SLABS, tm + 2 * HALO, LANES), F32),
                        pltpu.VMEM((SLABS, tm + 2 * HALO, LANES), F32),
                        pltpu.VMEM((tm, SEG), F32)],
        compiler_params=pltpu.CompilerParams(
            dimension_semantics=("parallel",), vmem_limit_bytes=VMEM_LIMIT_BYTES),
        name="in_proj",
    )(x2d, x2d, x2d, mod, mod, norm_g, w_in, *tabs, conv_dw, conv_dw_b, ln_g, ln_b)


def _edge_class_rows(rows):
    n_dr = 2 * NA_KH - 1
    idx = np.full((N_EDGE_CLASSES, 2, KEY_ROWS), n_dr, np.int32)
    for c, r0 in enumerate((0, 2, 4, rows - 4, rows - 2)):
        ks = int(np.clip(r0 - NA_KH // 2, 0, rows - KEY_ROWS))
        for i in range(2):
            r = r0 + i
            rs = int(np.clip(r - NA_KH // 2, 0, rows - NA_KH))
            for jr in range(KEY_ROWS):
                kr = ks + jr
                if rs <= kr < rs + NA_KH:
                    idx[c, i, jr] = kr - r + NA_KH - 1
    return idx


def _att_kernel(q_ref, k_ref, v_ref, kc_ref, vc_ref, bd_ref, o_ref, tab, *, rows):
    rb = pl.program_id(2)
    pair_q = 2 * GRID_W
    n_pairs = ATT_ROWS_PER_STEP // 2
    nt = (((1,), (1,)), ((), ()))

    @pl.when(rb == 0)
    def _():
        idx = _edge_class_rows(rows)
        low = lax.broadcasted_iota(jnp.int32, (GRID_W, LANES), 1) < GRID_W
        for c in range(N_EDGE_CLASSES):
            for hh in range(2):
                for i in range(2):
                    r_lo = (2 * hh + i) * GRID_W
                    for m in range(KEY_ROWS // 2):
                        a, b = int(idx[c, i, 2 * m]), int(idx[c, i, 2 * m + 1])
                        tab[c, r_lo:r_lo + GRID_W, m * LANES:(m + 1) * LANES] = LOG2E * jnp.where(
                            low, bd_ref[hh, a], bd_ref[hh, b])

    kc = kc_ref[...]
    vc = vc_ref[...]
    first = lax.broadcasted_iota(jnp.int32, (pair_q, LANES), 1) < HEAD_DIM

    def scores(pp):
        r0 = rb * ATT_ROWS_PER_STEP + 2 * pp
        ks = jnp.clip(r0 - NA_KH // 2, 0, rows - KEY_ROWS)
        start = pl.multiple_of(ks * GRID_W, LANES)
        cls = jnp.where(r0 == 0, 0, jnp.where(r0 == 2, 1, jnp.where(
            r0 == rows - 4, 3, jnp.where(r0 == rows - 2, 4, 2))))
        q2 = q_ref[pp * pair_q:(pp + 1) * pair_q, :]
        zero = jnp.zeros_like(q2)
        qs = jnp.concatenate([jnp.where(first, q2, zero), jnp.where(first, zero, q2)], axis=0)
        kw = k_ref[pl.ds(start, KEY_ROWS * GRID_W), :]
        s_loc = lax.dot_general(qs, kw, nt, preferred_element_type=F32) + tab[cls]
        s_ctx = lax.dot_general(qs, kc, nt, preferred_element_type=F32)
        return s_loc, s_ctx, start

    def softmax(s_loc, s_ctx):
        m = jnp.maximum(s_loc.max(axis=-1, keepdims=True), s_ctx.max(axis=-1, keepdims=True))
        e_loc = jnp.exp2(s_loc - m)
        e_ctx = jnp.exp2(s_ctx - m)
        denom = e_loc.sum(axis=-1, keepdims=True) + e_ctx.sum(axis=-1, keepdims=True)
        return e_loc.astype(BF16), e_ctx.astype(BF16), denom

    def weighted_values(pp, e_loc, e_ctx, denom, start):
        vw = v_ref[pl.ds(start, KEY_ROWS * GRID_W), :]
        o = (jnp.dot(e_loc, vw, preferred_element_type=F32)
             + jnp.dot(e_ctx, vc, preferred_element_type=F32)) / denom
        out = jnp.where(first, o[:pair_q], o[pair_q:])
        o_ref[pp * pair_q:(pp + 1) * pair_q, :] = out.astype(BF16)

    pending = scores(0)
    for pp in range(n_pairs):
        s_loc, s_ctx, start = pending
        if pp + 1 < n_pairs:
            pending = scores(pp + 1)
        weighted_values(pp, *softmax(s_loc, s_ctx), start)


def _att_call(qkv_lat, qkv_ctx, layer, bias_blocks, batch, seq_len, ctx_len):
    n_tok = qkv_lat.shape[1]
    rows = seq_len // GRID_W
    heads2 = (2 * SEG) // LANES
    rb_per_seq = rows // ATT_ROWS_PER_STEP
    tq = ATT_ROWS_PER_STEP * GRID_W
    n_dr = bias_blocks.shape[2]
    return pl.pallas_call(
        functools.partial(_att_kernel, rows=rows),
        grid=(batch, heads2, rb_per_seq),
        in_specs=[
            pl.BlockSpec((None, tq, LANES), lambda b, h, r: (Q_BLOCK + h, b * rb_per_seq + r, 0)),
            pl.BlockSpec((None, seq_len, LANES), lambda b, h, r: (K_BLOCK + h, b, 0)),
            pl.BlockSpec((None, seq_len, LANES), lambda b, h, r: (V_BLOCK + h, b, 0)),
            pl.BlockSpec((None, ctx_len, LANES), lambda b, h, r: (K_BLOCK + h, b, 0)),
            pl.BlockSpec((None, ctx_len, LANES), lambda b, h, r: (V_BLOCK + h, b, 0)),
            pl.BlockSpec((None, 2, n_dr, GRID_W, LANES), lambda b, h, r: (layer, h, 0, 0, 0)),
        ],
        out_specs=pl.BlockSpec((None, tq, LANES), lambda b, h, r: (h, b * rb_per_seq + r, 0)),
        out_shape=jax.ShapeDtypeStruct((heads2, n_tok, LANES), BF16),
        scratch_shapes=[pltpu.VMEM((N_EDGE_CLASSES, 4 * GRID_W, KEY_ROWS * GRID_W), F32)],
        compiler_params=pltpu.CompilerParams(
            dimension_semantics=("arbitrary", "arbitrary", "arbitrary"),
            vmem_limit_bytes=VMEM_LIMIT_BYTES),
        name="nbr_attention",
    )(qkv_lat, qkv_lat, qkv_lat, qkv_ctx, qkv_ctx, bias_blocks)


def _two_head_attention(q2, k2, v2):
    m_rows = q2.shape[0]
    first = lax.broadcasted_iota(jnp.int32, q2.shape, 1) < HEAD_DIM
    zero = jnp.zeros_like(q2)
    qs = jnp.concatenate([jnp.where(first, q2, zero), jnp.where(first, zero, q2)], axis=0)
    s = lax.dot_general(qs, k2, (((1,), (1,)), ((), ())), preferred_element_type=F32)
    e = jnp.exp2(s - s.max(axis=-1, keepdims=True))
    o = jnp.dot(e.astype(BF16), v2, preferred_element_type=F32) / e.sum(axis=-1, keepdims=True)
    return jnp.where(first, o[:m_rows], o[m_rows:])


def _ctx_att_kernel(q_ref, k_ref, v_ref, o_ref):
    o_ref[...] = _two_head_attention(q_ref[...], k_ref[...], v_ref[...]).astype(BF16)


def _ctx_att_call(qkv_ctx, batch, ctx_len):
    n_tok = qkv_ctx.shape[1]
    heads2 = (2 * SEG) // LANES
    return pl.pallas_call(
        _ctx_att_kernel,
        grid=(batch, heads2),
        in_specs=[
            pl.BlockSpec((None, ctx_len, LANES), lambda b, h: (Q_BLOCK + h, b, 0)),
            pl.BlockSpec((None, ctx_len, LANES), lambda b, h: (K_BLOCK + h, b, 0)),
            pl.BlockSpec((None, ctx_len, LANES), lambda b, h: (V_BLOCK + h, b, 0)),
        ],
        out_specs=pl.BlockSpec((None, ctx_len, LANES), lambda b, h: (h, b, 0)),
        out_shape=jax.ShapeDtypeStruct((heads2, n_tok, LANES), BF16),
        compiler_params=pltpu.CompilerParams(
            dimension_semantics=("parallel", "parallel"), vmem_limit_bytes=VMEM_LIMIT_BYTES),
        name="ctx_attention",
    )(qkv_ctx, qkv_ctx, qkv_ctx)


def _bias_blocks(na_rpb):
    depth, heads, n_dr, n_dc = na_rpb.shape
    cq = np.arange(GRID_W)[:, None]
    ck = np.arange(GRID_W)[None, :]
    cs = np.clip(cq - NA_KW // 2, 0, GRID_W - NA_KW)
    col_ok = (ck >= cs) & (ck < cs + NA_KW)
    dc = ck - cq + NA_KW - 1
    onehot = ((np.arange(n_dc)[:, None, None] == dc[None]) & col_ok[None]).astype(np.float32)
    onehot = np.concatenate([onehot, onehot], axis=-1)
    mask = np.where(np.concatenate([col_ok, col_ok], axis=-1), 0.0, MASK_VALUE).astype(np.float32)
    mask = np.concatenate([np.broadcast_to(mask, (n_dr,) + mask.shape),
                           np.full((1,) + mask.shape, MASK_VALUE, np.float32)], axis=0)
    rpb = jnp.concatenate([na_rpb.astype(F32), jnp.zeros((depth, heads, 1, n_dc), F32)], axis=2)
    return jnp.einsum("lhrd,dqk->lhrqk", rpb, jnp.asarray(onehot),
                      precision=lax.Precision.HIGHEST) + jnp.asarray(mask)


def _mix_kernel(x_ref, gate_ref, pooled_ref, ag_ref, at_ref, bg_ref, sw_ref, cg_ref,
                wpool_ref, ps_ref, pw_ref, pwb_ref, wout_ref, fg_ref,
                ng_ref, nscale_ref, nshift_ref, o_ref, *maybe_h_ref, final):
    ya = jnp.dot(pooled_ref[...], wpool_ref[...], preferred_element_type=F32)
    ya = ya * ps_ref[...] * ag_ref[...].astype(F32)
    cm = jnp.dot(sw_ref[...], pw_ref[...], preferred_element_type=F32) + pwb_ref[...]
    ycc = cm * cg_ref[...].astype(F32)
    att = jnp.concatenate([at_ref[hp] for hp in range(at_ref.shape[0])], axis=1)
    yb = att * bg_ref[...]
    cat = jnp.concatenate([ya.astype(BF16), yb, ycc.astype(BF16)], axis=1)
    mixed = jnp.dot(cat, wout_ref[...], preferred_element_type=F32)
    xn = x_ref[...] + gate_ref[...] * mixed
    if final:
        ms = jnp.mean(xn * xn, axis=-1, keepdims=True)
        xn = xn * lax.rsqrt(ms + NORM_EPS) * fg_ref[...]
    else:
        (h_ref,) = maybe_h_ref
        ms = jnp.mean(xn * xn, axis=-1, keepdims=True)
        gain = ng_ref[...] * (1.0 + nscale_ref[...])
        h_ref[...] = (xn * lax.rsqrt(ms + NORM_EPS) * gain + nshift_ref[...]).astype(BF16)
    o_ref[...] = xn


def _mix_call(x2d, layer, mod, mod_row, p, att, wpool, pool_scale, conv_pw, conv_pw_b, w_out,
              final_g, norm_g, seq_len, tm, final):
    n_tok, d = x2d.shape
    seq_tiles = seq_len // tm
    nl = min(layer + 1, norm_g.shape[0] - 1)
    x_spec = pl.BlockSpec((tm, d), lambda i: (i, 0))
    out_specs = [x_spec] if final else [x_spec, x_spec]
    out_shape = [jax.ShapeDtypeStruct((n_tok, d), F32)]
    if not final:
        out_shape.append(jax.ShapeDtypeStruct((n_tok, d), BF16))
    if mod_row is None:
        row = lambda i: i // seq_tiles
    else:
        row = lambda i: mod_row

    def group(colblk):
        return pl.BlockSpec((tm, SEG), lambda i: (i, colblk))

    return pl.pallas_call(
        functools.partial(_mix_kernel, final=final),
        grid=(n_tok // tm,),
        in_specs=[
            pl.BlockSpec((tm, d), lambda i: (i, 0)),
            pl.BlockSpec((None, None, 1, d), lambda i: (layer, row(i), 0, 2)),
            group(0),
            group(1),
            pl.BlockSpec((att.shape[0], tm, LANES), lambda i: (0, i, 0)),
            pl.BlockSpec((tm, 2 * SEG), lambda i: (i, 1)),
            group(4),
            group(5),
            _layer_spec(wpool, layer, 1), _layer_spec(pool_scale, layer, 1),
            _layer_spec(conv_pw, layer, 1), _layer_spec(conv_pw_b, layer, 1),
            _layer_spec(w_out, layer, 1),
            pl.BlockSpec(final_g.shape, lambda i: (0, 0), pipeline_mode=pl.Buffered(1)),
            _layer_spec(norm_g, nl, 1),
            pl.BlockSpec((None, None, 1, d), lambda i: (nl, row(i), 0, 1)),
            pl.BlockSpec((None, None, 1, d), lambda i: (nl, row(i), 0, 0)),
        ],
        out_specs=out_specs,
        out_shape=out_shape,
        compiler_params=pltpu.CompilerParams(
            dimension_semantics=("parallel",), vmem_limit_bytes=VMEM_LIMIT_BYTES),
        name="mix_out",
    )(x2d, mod, p, p, att, p, p, p, wpool, pool_scale, conv_pw, conv_pw_b, w_out, final_g,
      norm_g, mod, mod)


def _rope_tables(seq_len):
    half = HEAD_DIM // 2
    freqs = ROPE_THETA ** (-jnp.arange(0, half, 2, dtype=F32) / half)
    t = jnp.arange(seq_len)
    ang_r = (t // GRID_W).astype(F32)[:, None] * freqs[None, :]
    ang_c = (t % GRID_W).astype(F32)[:, None] * freqs[None, :]
    cos = jnp.concatenate([jnp.cos(ang_r)] * 2 + [jnp.cos(ang_c)] * 2, axis=-1)
    sin = jnp.concatenate([-jnp.sin(ang_r), jnp.sin(ang_r), -jnp.sin(ang_c), jnp.sin(ang_c)], axis=-1)
    cos = jnp.tile(cos, (1, LANES // HEAD_DIM))
    sin = jnp.tile(sin, (1, LANES // HEAD_DIM))
    return cos * Q_SCALE, sin * Q_SCALE, cos, sin


def kernel(x, c, ctx, c_ctx, w_mod, b_mod, norm_g, w_in, w_pool, pool_scale, na_rpb, conv_dw,
           conv_dw_b, conv_ln_g, conv_ln_b, conv_pw, conv_pw_b, w_out, final_norm_g):
    batch, seq_len, d = x.shape
    ctx_len = ctx.shape[1]
    depth = w_mod.shape[0]
    assert d == 4 * SEG and w_in.shape[2] == N_IN_GROUPS * SEG
    assert seq_len % (ATT_ROWS_PER_STEP * GRID_W) == 0 and seq_len // GRID_W >= 12
    assert batch + 1 <= 8 and w_pool.shape[-1] == LANES

    cstack = jnp.zeros((8, d), F32).at[:batch].set(c).at[batch].set(c_ctx)
    mod = _mod_call(cstack, w_mod, b_mod).reshape(depth, 8, 1, 3 * d)

    lat_tabs = _rope_tables(seq_len)
    ones = jnp.ones((ctx_len, LANES), F32)
    ctx_tabs = (ones * Q_SCALE, jnp.zeros_like(ones), ones, jnp.zeros_like(ones))
    bias_blocks = _bias_blocks(na_rpb)

    groups = len(POOL_WINDOWS)
    pg = w_pool.shape[-1]
    eye = jnp.eye(groups, dtype=F32)
    wpool_bd = (eye[None, :, None, :, None] * w_pool[:, :, :, None, :]).reshape(
        depth, groups * pg, groups * pg).astype(BF16)
    w_in_b = w_in.astype(BF16)
    w_out_b = w_out.astype(BF16)
    conv_pw_b16 = conv_pw.astype(BF16)
    row_vec = lambda a: a.reshape(depth, 1, a.shape[-1])
    in_w = (row_vec(norm_g), w_in_b)
    conv_w = (conv_dw, row_vec(conv_dw_b), row_vec(conv_ln_g), row_vec(conv_ln_b))
    mix_w = (wpool_bd, row_vec(pool_scale), conv_pw_b16, row_vec(conv_pw_b), w_out_b,
             final_norm_g.reshape(1, d), row_vec(norm_g))

    tm_in = min(256, seq_len)
    tm_mix = min(512, seq_len)
    xl = x.reshape(batch * seq_len, d)
    xc = ctx.reshape(batch * ctx_len, d)
    hl, hc = xl, xc
    for i in range(depth):
        last = i == depth - 1
        p_ctx, qkv_ctx = _in_call(hc, i, mod, batch, *in_w, ctx_tabs, *conv_w, ctx_len,
                                  min(tm_in, ctx_len), prenormed=i > 0)
        p_lat, qkv_lat = _in_call(hl, i, mod, None, *in_w, lat_tabs, *conv_w, seq_len, tm_in,
                                  prenormed=i > 0)
        att = _att_call(qkv_lat, qkv_ctx, i, bias_blocks, batch, seq_len, ctx_len)
        if last:
            (xl,) = _mix_call(xl, i, mod, None, p_lat, att, *mix_w, seq_len, tm_mix, True)
        else:
            xl, hl = _mix_call(xl, i, mod, None, p_lat, att, *mix_w, seq_len, tm_mix, False)
            att_c = _ctx_att_call(qkv_ctx, batch, ctx_len)
            xc, hc = _mix_call(xc, i, mod, batch, p_ctx, att_c, *mix_w, ctx_len, ctx_len, False)
    return xl.reshape(batch, seq_len, d)
```
